```python
import math
import jax, jax.numpy as jnp
from jax import lax
import numpy as np

D_MODEL = 1024
BATCH = 8
SEQ = 8192
DEPTH = 4

HEAD_DIM = 64
GROUP_W = D_MODEL // 4
HEADS_PER_MIXER = GROUP_W // HEAD_DIM
MIX_W = 4 * GROUP_W
NORM_EPS = 1e-6
GDN_CONV = 4
GDN_CHUNK = 64
SGU_CHUNK = 128
SGU_LN_EPS = 1e-5
SC_CONV = 3
RW_DECAY_LORA = 64
RW_AAA_LORA = 64
RW_GATE_LORA = 128
RW_GN_EPS = 64e-5
RW_IN_W = 3 * GROUP_W + RW_DECAY_LORA + RW_AAA_LORA + RW_GATE_LORA
IN_SIZES = (3 * GROUP_W, GROUP_W, HEADS_PER_MIXER, HEADS_PER_MIXER, GROUP_W, GROUP_W,
            GROUP_W, GROUP_W, GROUP_W, RW_IN_W)
IN_W = sum(IN_SIZES)
N_GROUPS = 8
EXPERTS_PER_GROUP = 8
N_EXPERTS = N_GROUPS * EXPERTS_PER_GROUP
TOP_K = 2
D_EXPERT = 256
MOE_BLOCK = 128

kernel_name = "hybrid_headgroup_gdn_sgu_conv_rwkv7_hmoe"

F32 = jnp.float32


def _split(p, sizes):
    idx, s = [], 0
    for n in sizes[:-1]:
        s += n
        idx.append(s)
    return jnp.split(p, idx, axis=-1)


def rmsnorm(x, g):
    xf = x.astype(F32)
    y = xf * lax.rsqrt(jnp.mean(xf * xf, -1, keepdims=True) + NORM_EPS)
    return (y * g.astype(F32)).astype(x.dtype)


def modulate(h, shift, scale):
    return h * (1.0 + scale[:, None, :]) + shift[:, None, :]


def causal_conv(x, w):
    k = w.shape[0]
    return lax.conv_general_dilated(x, w[:, None, :].astype(x.dtype), window_strides=(1,),
                                    padding=[(k - 1, 0)], dimension_numbers=('NWC', 'WIO', 'NWC'),
                                    feature_group_count=x.shape[-1])


def token_shift(x):
    return jnp.pad(x[:, :-1], ((0, 0), (1, 0), (0, 0)))


def l2norm(x):
    return x * lax.rsqrt(jnp.sum(x * x, -1, keepdims=True) + 1e-6)


def gated_delta_chunked(q, k, v, g, beta):
    bn, s, h, dh = q.shape
    c = GDN_CHUNK
    n = s // c

    def chunks(t):
        return jnp.moveaxis(t.reshape(bn, n, c, h, -1), 3, 1)

    q = chunks(q) * (dh ** -0.5)
    k = chunks(k)
    v = chunks(v)
    gc = jnp.cumsum(jnp.moveaxis(g.reshape(bn, n, c, h), 3, 1), axis=-1)
    beta = jnp.moveaxis(beta.reshape(bn, n, c, h), 3, 1)
    pos = jnp.arange(c)
    causal = pos[:, None] >= pos[None, :]
    strict = pos[:, None] > pos[None, :]
    decay = jnp.exp(jnp.where(causal, gc[..., :, None] - gc[..., None, :], -jnp.inf))
    k_beta = k * beta[..., None]
    a_mat = jnp.where(strict, jnp.einsum('bhnid,bhnjd->bhnij', k_beta, k) * decay, 0.0)
    rhs = jnp.concatenate([v * beta[..., None], k_beta * jnp.exp(gc)[..., None]], -1)
    sol = lax.linalg.triangular_solve(a_mat, rhs, left_side=True, lower=True, unit_diagonal=True)
    u, w = sol[..., :dh], sol[..., dh:]
    attn = jnp.einsum('bhnid,bhnjd->bhnij', q, k) * decay
    q_dec = q * jnp.exp(gc)[..., None]
    k_tail = k * jnp.exp(gc[..., -1:] - gc)[..., None]
    chunk_dec = jnp.exp(gc[..., -1])

    def step(state, inp):
        u_i, w_i, a_i, qd_i, kt_i, cd_i = inp
        v_new = u_i - jnp.einsum('bhck,bhkv->bhcv', w_i, state)
        o = jnp.einsum('bhck,bhkv->bhcv', qd_i, state) + jnp.einsum('bhij,bhjv->bhiv', a_i, v_new)
        state = state * cd_i[..., None, None] + jnp.einsum('bhck,bhcv->bhkv', kt_i, v_new)
        return state, o

    xs = tuple(jnp.moveaxis(t, 2, 0) for t in (u, w, attn, q_dec, k_tail, chunk_dec))
    _, o = lax.scan(step, jnp.zeros((bn, h, dh, dh), F32), xs)
    return jnp.transpose(o, (1, 0, 3, 2, 4)).reshape(bn, s, h, dh)


def gdn_mixer(qkv, z, a, b, conv_w, a_log, dt_bias, norm_g):
    bn, s, _ = qkv.shape
    h, n = HEADS_PER_MIXER, HEAD_DIM
    qkv = jax.nn.silu(causal_conv(qkv, conv_w)).astype(F32)
    q, k, v = (t.reshape(bn, s, h, n) for t in jnp.split(qkv, 3, axis=-1))
    q = l2norm(q)
    k = l2norm(k)
    g = -jnp.exp(a_log.astype(F32)) * jax.nn.softplus(a.astype(F32) + dt_bias.astype(F32))
    beta = jax.nn.sigmoid(b.astype(F32))
    o = gated_delta_chunked(q, k, v, g, beta)
    o = o * lax.rsqrt(jnp.mean(o * o, -1, keepdims=True) + NORM_EPS) * norm_g.astype(F32)
    o = o.reshape(bn, s, GROUP_W) * jax.nn.silu(z.astype(F32))
    return o.astype(z.dtype)


def sgu_mixer(u, v, ln_g, ln_b, w_s, b_s):
    bn, s, _ = u.shape
    t, h = SGU_CHUNK, HEADS_PER_MIXER
    u = jax.nn.gelu(u)
    vf = jax.nn.gelu(v).astype(F32)
    mean = jnp.mean(vf, -1, keepdims=True)
    var = jnp.mean(jnp.square(vf - mean), -1, keepdims=True)
    v = ((vf - mean) * lax.rsqrt(var + SGU_LN_EPS) * ln_g.astype(F32) + ln_b.astype(F32)).astype(u.dtype)
    v = v.reshape(bn, s // t, t, h, HEAD_DIM)
    ws = jnp.where(jnp.tril(jnp.ones((t, t), bool)), w_s, 0.0).astype(u.dtype)
    mixed = jnp.einsum('hts,bnshc->bnthc', ws, v) + b_s.T[None, None, :, :, None]
    return (u.reshape(bn, s // t, t, h, HEAD_DIM) * mixed).reshape(bn, s, GROUP_W)


def short_conv_mixer(gate_b, gate_c, hx, conv_w):
    return gate_b * causal_conv(gate_c * hx, conv_w)


def rwkv7_scan(r, w, k, v, za, zb):
    bn, s, h, n = r.shape
    xs = tuple(jnp.moveaxis(t, 1, 0) for t in (r, w, k, v, za, zb))

    def step(state, inp):
        r_t, w_t, k_t, v_t, a_t, b_t = inp
        sa = jnp.einsum('bhvk,bhk->bhv', state, a_t)
        state = (state * w_t[:, :, None, :] + sa[..., None] * b_t[:, :, None, :]
                 + v_t[..., None] * k_t[:, :, None, :])
        return state, jnp.einsum('bhvk,bhk->bhv', state, r_t)

    _, y = lax.scan(step, jnp.zeros((bn, h, n, n), F32), xs)
    return jnp.moveaxis(y, 0, 1)


def rwkv7_mixer(p, mu, w0, w_up, a0, a_up, g_up, k_k, k_a, r_k, gn_g, gn_b):
    bn, s, _ = p.shape
    g_w, h, n = GROUP_W, HEADS_PER_MIXER, HEAD_DIM
    p = p + (token_shift(p) - p) * mu
    r, k, v, xw, xa, xg = _split(p, (g_w, g_w, g_w, RW_DECAY_LORA, RW_AAA_LORA, RW_GATE_LORA))
    w_log = -jax.nn.softplus(-(w0 + jnp.tanh(xw) @ w_up).astype(F32)) - 0.5
    decay = jnp.exp(-jnp.exp(w_log))
    a = jax.nn.sigmoid((a0 + xa @ a_up).astype(F32))
    gate = jax.nn.sigmoid(xg) @ g_up

    def heads(t):
        return t.astype(F32).reshape(bn, s, h, n)

    kf = k.astype(F32)
    kk = heads(kf * k_k.astype(F32))
    kk = kk * lax.rsqrt(jnp.sum(kk * kk, -1, keepdims=True) + 1e-12)
    k_mod = heads(kf * (1.0 + (a - 1.0) * k_a.astype(F32)))
    a_h = heads(a)
    r_h = heads(r)
    v_h = heads(v)
    y = rwkv7_scan(r_h, heads(decay), k_mod, v_h, -kk, kk * a_h)
    mean = jnp.mean(y, -1, keepdims=True)
    var = jnp.mean(jnp.square(y - mean), -1, keepdims=True)
    y = (y - mean) * lax.rsqrt(var + RW_GN_EPS)
    y = y * gn_g.astype(F32).reshape(h, n) + gn_b.astype(F32).reshape(h, n)
    y = y + jnp.sum(r_h * k_mod * r_k.astype(F32), -1, keepdims=True) * v_h
    return (y.reshape(bn, s, g_w) * gate.astype(F32)).astype(p.dtype)


def expert_dispatch(h, eidx, gates, w_gate, w_up, w_down):
    t, d = h.shape
    a_n = t * TOP_K
    flat_e = eidx.reshape(a_n)
    flat_tok = jnp.repeat(jnp.arange(t, dtype=jnp.int32), TOP_K)
    flat_g = gates.reshape(a_n)
    order = jnp.argsort(flat_e)
    se = flat_e[order]
    counts = jnp.bincount(flat_e, length=N_EXPERTS)
    start = jnp.cumsum(counts) - counts
    padded = (counts + MOE_BLOCK - 1) // MOE_BLOCK * MOE_BLOCK
    pad_end = jnp.cumsum(padded)
    pad_start = pad_end - padded
    dest = pad_start[se] + jnp.arange(a_n, dtype=jnp.int32) - start[se]
    p_rows = (a_n + MOE_BLOCK - 1) // MOE_BLOCK * MOE_BLOCK + N_EXPERTS * MOE_BLOCK
    n_blk = p_rows // MOE_BLOCK
    row_tok = jnp.full((p_rows,), t, jnp.int32).at[dest].set(flat_tok[order])
    row_g = jnp.zeros((p_rows,), h.dtype).at[dest].set(flat_g[order])
    blk_e = jnp.minimum(jnp.searchsorted(pad_end, jnp.arange(n_blk) * MOE_BLOCK, side='right'),
                        N_EXPERTS - 1)
    h_pad = jnp.concatenate([h, jnp.zeros((1, d), h.dtype)], 0)

    def run(args):
        tok, e, g = args
        xb = h_pad[tok]
        hid = jax.nn.silu(xb @ w_gate[e]) * (xb @ w_up[e])
        return (hid @ w_down[e]) * g[:, None]

    y = lax.map(run, (row_tok.reshape(n_blk, MOE_BLOCK), blk_e, row_g.reshape(n_blk, MOE_BLOCK)))
    out = jnp.zeros((t + 1, d), h.dtype).at[row_tok].add(y.reshape(p_rows, d))
    return out[:t]


def hier_moe(h, w_group, b_group, w_router, b_router, w_gate, w_up, w_down):
    bn, s, d = h.shape
    t = bn * s
    hf = h.reshape(t, d)
    glog = (hf @ w_group + b_group).astype(F32)
    gprob = jax.nn.softmax(glog, axis=-1)
    gsel = jnp.argmax(glog, axis=-1).astype(jnp.int32)
    p_group = jnp.take_along_axis(gprob, gsel[:, None], axis=-1)
    elog = (hf @ w_router + b_router).astype(F32).reshape(t, N_GROUPS, EXPERTS_PER_GROUP)
    elog = jnp.take_along_axis(elog, gsel[:, None, None], axis=1)[:, 0]
    top_v, top_i = lax.top_k(elog, TOP_K)
    gates = jax.nn.softmax(top_v, axis=-1) * p_group
    eidx = gsel[:, None] * EXPERTS_PER_GROUP + top_i.astype(jnp.int32)
    y = expert_dispatch(hf, eidx, gates.astype(h.dtype), w_gate, w_up, w_down)
    return y.reshape(bn, s, d)


def setup_inputs(seed: int = 0) -> dict:
    key = jax.random.key(seed)
    keys = jax.random.split(key, 48)
    cnt = [0]
    L, D, G, H, N = DEPTH, D_MODEL, GROUP_W, HEADS_PER_MIXER, HEAD_DIM

    def nk():
        cnt[0] += 1
        return keys[cnt[0] - 1]

    def nrm(shape, scale):
        return jax.random.normal(nk(), shape, F32) * scale

    def gain(shape):
        return 1.0 + nrm(shape, 0.02)

    def unif(shape, lo, hi):
        return jax.random.uniform(nk(), shape, F32, lo, hi)

    x = nrm((BATCH, SEQ, D), 1.0)
    c = nrm((BATCH, D), 1.0)
    ada_w = nrm((L, D, 6 * D), 0.5 * D ** -0.5)
    ada_b = nrm((L, 6 * D), 0.02)
    mix_norm_g = gain((L, D))
    ffn_norm_g = gain((L, D))
    w_in = nrm((L, D, IN_W), D ** -0.5)
    w_out = nrm((L, MIX_W, D), MIX_W ** -0.5)
    gdn_conv_w = nrm((L, GDN_CONV, 3 * G), GDN_CONV ** -0.5)
    gdn_a_log = jnp.log(unif((L, H), 1.0, 16.0))
    dt = jnp.exp(unif((L, H), math.log(1e-3), math.log(1e-1)))
    gdn_dt_bias = dt + jnp.log(-jnp.expm1(-dt))
    gdn_norm_g = gain((L, N))
    sgu_ln_g = gain((L, G))
    sgu_ln_b = nrm((L, G), 0.02)
    sgu_w = nrm((L, H, SGU_CHUNK, SGU_CHUNK), SGU_CHUNK ** -0.5)
    sgu_b = gain((L, H, SGU_CHUNK))
    sc_conv_w = nrm((L, SC_CONV, G), SC_CONV ** -0.5)
    rw_mu = unif((L, RW_IN_W), 0.0, 1.0)
    rw_w0 = unif((L, G), -4.0, 1.0)
    rw_w_up = nrm((L, RW_DECAY_LORA, G), 0.5 * RW_DECAY_LORA ** -0.5)
    rw_a0 = nrm((L, G), 0.5)
    rw_a_up = nrm((L, RW_AAA_LORA, G), RW_AAA_LORA ** -0.5)
    rw_g_up = nrm((L, RW_GATE_LORA, G), RW_GATE_LORA ** -0.5)
    rw_k_k = 0.85 + nrm((L, G), 0.02)
    rw_k_a = gain((L, G))
    rw_r_k = nrm((L, H, N), 0.1)
    rw_gn_g = gain((L, G))
    rw_gn_b = nrm((L, G), 0.02)
    moe_w_group = nrm((L, D, N_GROUPS), D ** -0.5)
    moe_b_group = nrm((L, N_GROUPS), 0.01)
    moe_w_router = nrm((L, D, N_EXPERTS), D ** -0.5)
    moe_b_router = nrm((L, N_EXPERTS), 0.01)
    moe_w_gate = nrm((L, N_EXPERTS, D, D_EXPERT), D ** -0.5)
    moe_w_up = nrm((L, N_EXPERTS, D, D_EXPERT), D ** -0.5)
    moe_w_down = nrm((L, N_EXPERTS, D_EXPERT, D), D_EXPERT ** -0.5)
    final_norm_g = gain((D,))
    return {'x': x, 'c': c, 'ada_w': ada_w, 'ada_b': ada_b, 'mix_norm_g': mix_norm_g,
            'ffn_norm_g': ffn_norm_g, 'w_in': w_in, 'w_out': w_out, 'gdn_conv_w': gdn_conv_w,
            'gdn_a_log': gdn_a_log, 'gdn_dt_bias': gdn_dt_bias, 'gdn_norm_g': gdn_norm_g,
            'sgu_ln_g': sgu_ln_g, 'sgu_ln_b': sgu_ln_b, 'sgu_w': sgu_w, 'sgu_b': sgu_b,
            'sc_conv_w': sc_conv_w, 'rw_mu': rw_mu, 'rw_w0': rw_w0, 'rw_w_up': rw_w_up,
            'rw_a0': rw_a0, 'rw_a_up': rw_a_up, 'rw_g_up': rw_g_up, 'rw_k_k': rw_k_k,
            'rw_k_a': rw_k_a, 'rw_r_k': rw_r_k, 'rw_gn_g': rw_gn_g, 'rw_gn_b': rw_gn_b,
            'moe_w_group': moe_w_group, 'moe_b_group': moe_b_group, 'moe_w_router': moe_w_router,
            'moe_b_router': moe_b_router, 'moe_w_gate': moe_w_gate, 'moe_w_up': moe_w_up,
            'moe_w_down': moe_w_down, 'final_norm_g': final_norm_g}


def reference(x, c, ada_w, ada_b, mix_norm_g, ffn_norm_g, w_in, w_out, gdn_conv_w, gdn_a_log,
              gdn_dt_bias, gdn_norm_g, sgu_ln_g, sgu_ln_b, sgu_w, sgu_b, sc_conv_w, rw_mu, rw_w0,
              rw_w_up, rw_a0, rw_a_up, rw_g_up, rw_k_k, rw_k_a, rw_r_k, rw_gn_g, rw_gn_b,
              moe_w_group, moe_b_group, moe_w_router, moe_b_router, moe_w_gate, moe_w_up,
              moe_w_down, final_norm_g):
    c_act = jax.nn.silu(c)
    for l in range(DEPTH):
        mod = c_act @ ada_w[l] + ada_b[l]
        sh_m, sc_m, gt_m, sh_f, sc_f, gt_f = jnp.split(mod, 6, axis=-1)
        h = modulate(rmsnorm(x, mix_norm_g[l]), sh_m, sc_m)
        p = h @ w_in[l]
        qkv, z, ga, gb, su, sv, cb, cc, ch, rp = _split(p, IN_SIZES)
        o_a = gdn_mixer(qkv, z, ga, gb, gdn_conv_w[l], gdn_a_log[l], gdn_dt_bias[l], gdn_norm_g[l])
        o_b = sgu_mixer(su, sv, sgu_ln_g[l], sgu_ln_b[l], sgu_w[l], sgu_b[l])
        o_c = short_conv_mixer(cb, cc, ch, sc_conv_w[l])
        o_d = rwkv7_mixer(rp, rw_mu[l], rw_w0[l], rw_w_up[l], rw_a0[l], rw_a_up[l], rw_g_up[l],
                          rw_k_k[l], rw_k_a[l], rw_r_k[l], rw_gn_g[l], rw_gn_b[l])
        mixed = jnp.concatenate([o_a, o_b, o_c, o_d], axis=-1) @ w_out[l]
        x = x + gt_m[:, None, :] * mixed
        h = modulate(rmsnorm(x, ffn_norm_g[l]), sh_f, sc_f)
        y = hier_moe(h, moe_w_group[l], moe_b_group[l], moe_w_router[l], moe_b_router[l],
                     moe_w_gate[l], moe_w_up[l], moe_w_down[l])
        x = x + gt_f[:, None, :] * y
    return rmsnorm(x, final_norm_g)
```

```python
import functools
import math

import jax
import jax.numpy as jnp
from jax import lax
from jax.experimental import pallas as pl
from jax.experimental.pallas import tpu as pltpu

F32 = jnp.float32
BF16 = jnp.bfloat16

HEAD_DIM = 64
N_HEADS = 4
GROUP_W = HEAD_DIM * N_HEADS
CHUNK = 64
TILE = 2 * CHUNK
NORM_EPS = 1e-6
SGU_CHUNK = 128
SGU_LN_EPS = 1e-5
RW_GN_EPS = 64e-5
RW_LORA_W, RW_LORA_A, RW_LORA_G = 64, 64, 128
N_GROUPS = 8
EXPERTS_PER_GROUP = 8
N_EXPERTS = N_GROUPS * EXPERTS_PER_GROUP
D_EXPERT = 256
LANES = 128
VMEM_LIMIT = 56 * 1024 * 1024

ROW_BLOCK = 512
MOE_ROWS = 256
MOE_TOK = 256


def _dot(a, b):
    return jnp.dot(a.astype(BF16), b.astype(BF16), preferred_element_type=F32)


def _dot_nt(a, b):
    return lax.dot_general(a.astype(BF16), b.astype(BF16), (((1,), (1,)), ((), ())),
                           preferred_element_type=F32)


def _dot_tn(a, b):
    return lax.dot_general(a.astype(BF16), b.astype(BF16), (((0,), (0,)), ((), ())),
                           preferred_element_type=F32)


def _split2(x):
    hi = x.astype(BF16)
    lo = (x - hi.astype(F32)).astype(BF16)
    return hi, lo


def _dot_x_exact(x, m):
    hi, lo = _split2(x)
    return (jnp.dot(hi, m, preferred_element_type=F32) + jnp.dot(lo, m, preferred_element_type=F32))


def _dot_exact_x(m, x):
    hi, lo = _split2(x)
    return (jnp.dot(m, hi, preferred_element_type=F32) + jnp.dot(m, lo, preferred_element_type=F32))


def _sigmoid(x):
    return 1.0 / (1.0 + jnp.exp(-x))


def _silu(x):
    return x * _sigmoid(x)


def _softplus(x):
    return jnp.maximum(x, 0.0) + jnp.log(1.0 + jnp.exp(-jnp.abs(x)))


def _head_masks():
    lane = lax.broadcasted_iota(jnp.int32, (1, GROUP_W), 1)
    return [((lane >> 6) == h).astype(F32) for h in range(N_HEADS)]


def _tile_masks():
    ri = lax.broadcasted_iota(jnp.int32, (TILE, TILE), 0)
    ci = lax.broadcasted_iota(jnp.int32, (TILE, TILE), 1)
    same = (ri >> 6) == (ci >> 6)
    return same & (ri > ci), same & (ri >= ci), (ri == ci).astype(F32)


def _block_diag_mask():
    ri = lax.broadcasted_iota(jnp.int32, (GROUP_W, GROUP_W), 0)
    ci = lax.broadcasted_iota(jnp.int32, (GROUP_W, GROUP_W), 1)
    return ((ri >> 6) == (ci >> 6)).astype(F32)


def _unit_lower_inverse(x, eye):
    p = eye + x
    xp = x
    for _ in range(5):
        xp = _dot(xp, xp)
        p = p + _dot(p, xp)
    return p


def _seg_sum(x, seg):
    return _dot_x_exact(x, seg)


def _cparams(sem):
    return pltpu.CompilerParams(dimension_semantics=sem, vmem_limit_bytes=VMEM_LIMIT)


def _ada_kernel(c_ref, w_ref, b_ref, o_ref):
    c = c_ref[...]
    ca = _silu(c)
    chi, clo = _split2(ca)
    w = w_ref[...]
    whi, wlo = _split2(w)
    acc = jnp.dot(chi, whi, preferred_element_type=F32)
    acc += jnp.dot(clo, whi, preferred_element_type=F32)
    acc += jnp.dot(chi, wlo, preferred_element_type=F32)
    o_ref[...] = acc + b_ref[...]


def _ada(c, ada_w, ada_b):
    depth, d, d6 = ada_w.shape
    bn = c.shape[0]
    nj = d6 // d
    return pl.pallas_call(
        _ada_kernel,
        out_shape=jax.ShapeDtypeStruct((depth, bn, d6), F32),
        grid=(depth, nj),
        in_specs=[pl.BlockSpec((bn, d), lambda l, j: (0, 0)),
                  pl.BlockSpec((None, d, d), lambda l, j: (l, 0, j)),
                  pl.BlockSpec((None, 1, d), lambda l, j: (l, 0, j))],
        out_specs=pl.BlockSpec((None, bn, d), lambda l, j: (l, 0, j)),
        compiler_params=_cparams(("arbitrary", "arbitrary")),
        name="ada_mod",
    )(c, ada_w, ada_b.reshape(depth, 1, d6))


def _in_proj_kernel(x_ref, g_ref, sh_ref, sc_ref, w_ref, wab_ref, pg_ref, psc_ref, prw_ref, pab_ref, abt_ref):
    x = x_ref[...]
    y = x * lax.rsqrt(jnp.mean(x * x, -1, keepdims=True) + NORM_EPS) * g_ref[...]
    h = (y * (1.0 + sc_ref[0]) + sh_ref[0]).astype(BF16)
    o = 0
    for ref in (pg_ref, psc_ref, prw_ref, pab_ref):
        w = ref.shape[1]
        ref[...] = jnp.dot(h, w_ref[:, o:o + w], preferred_element_type=F32)
        o += w
    abt_ref[...] = lax.dot_general(wab_ref[...], h, (((1,), (1,)), ((), ())), preferred_element_type=F32)


def _in_proj(x, g, shift, scale, w_r, w_abt, seq):
    t, d = x.shape
    tm = ROW_BLOCK
    per_b = seq // tm
    widths = (4 * GROUP_W, 5 * GROUP_W, 4 * GROUP_W, LANES)
    bspec = pl.BlockSpec((1, 1, d), lambda i: (i // per_b, 0, 0))
    return pl.pallas_call(
        _in_proj_kernel,
        out_shape=tuple(jax.ShapeDtypeStruct((t, w), F32) for w in widths) + (jax.ShapeDtypeStruct((8, t), F32),),
        grid=(t // tm,),
        in_specs=[pl.BlockSpec((tm, d), lambda i: (i, 0)),
                  pl.BlockSpec((1, d), lambda i: (0, 0)),
                  bspec, bspec,
                  pl.BlockSpec(w_r.shape, lambda i: (0, 0)),
                  pl.BlockSpec(w_abt.shape, lambda i: (0, 0))],
        out_specs=tuple(pl.BlockSpec((tm, w), lambda i: (i, 0)) for w in widths) + (pl.BlockSpec((8, tm), lambda i: (0, i)),),
        compiler_params=_cparams(("arbitrary",)),
        name="in_proj",
    )(x, g, shift, scale, w_r, w_abt)


def _gdn_kernel(p_ref, ab_ref, abt_ref, cw_ref, alog_ref, dtb_ref, alogt_ref, dtbt_ref, ng_ref,
                eg_ref, eb_ref, egx_ref, seg_ref, tri_ref, trit_ref, full_ref,
                o_ref,
                xbuf, s_ref, q_s, k_s, kb_s, rhs_s, qd_s, kt_s, cd_s, gcx_s, gct_s):
    rows = p_ref.shape[0]
    j = pl.program_id(1)

    @pl.when(j == 0)
    def _():
        xbuf[0:8, :] = jnp.zeros((8, xbuf.shape[1]), F32)
        s_ref[...] = jnp.zeros(s_ref.shape, F32)

    xbuf[8:8 + rows, :] = p_ref[:, 0:3 * GROUP_W]
    acc = cw_ref[3:4, :] * xbuf[8:8 + rows, :]
    for tap in range(3):
        acc = acc + cw_ref[tap:tap + 1, :] * xbuf[5 + tap:5 + tap + rows, :]
    xbuf[0:8, :] = xbuf[rows:rows + 8, :]
    qkv = _silu(acc)
    seg = seg_ref[...]
    q = qkv[:, 0:GROUP_W]
    k = qkv[:, GROUP_W:2 * GROUP_W]
    v = qkv[:, 2 * GROUP_W:3 * GROUP_W]
    q = q * lax.rsqrt(_seg_sum(q * q, seg) + 1e-6) * (HEAD_DIM ** -0.5)
    k = k * lax.rsqrt(_seg_sum(k * k, seg) + 1e-6)

    ab = ab_ref[...]
    g = -jnp.exp(alog_ref[...]) * _softplus(ab + dtb_ref[...])
    beta = _dot_x_exact(_sigmoid(ab), eb_ref[...])
    gc = _dot_exact_x(tri_ref[...], _dot_x_exact(g, eg_ref[...]))
    gl = _dot_exact_x(full_ref[...], _dot_x_exact(g, eg_ref[...]))
    egc = jnp.exp(gc)
    kb = k * beta
    q_s[...] = q
    k_s[...] = k
    kb_s[...] = kb
    rhs_s[:, 0:GROUP_W] = v * beta
    rhs_s[:, GROUP_W:2 * GROUP_W] = kb * egc
    qd_s[...] = q * egc
    kt_s[...] = k * jnp.exp(gl - gc)
    cd_s[...] = jnp.exp(gl)
    gcx_s[...] = _dot_exact_x(tri_ref[...], _dot_x_exact(g, egx_ref[...]))
    abt = abt_ref[...]
    gt = -jnp.exp(alogt_ref[...]) * _softplus(abt + dtbt_ref[...])
    gct = _dot_x_exact(gt, trit_ref[...])
    for t in range(rows // TILE):
        gct_s[t] = gct[:, t * TILE:(t + 1) * TILE]

    hm = _head_masks()
    strict, incl, eye = _tile_masks()
    bd = _block_diag_mask()
    z_all = p_ref[:, 3 * GROUP_W:4 * GROUP_W]
    ng = ng_ref[...]

    def tile_body(t, carry):
        r0 = pl.multiple_of(t * TILE, TILE)
        rs = pl.ds(r0, TILE)
        qt, kt_, kbt, rhs = q_s[rs, :], k_s[rs, :], kb_s[rs, :], rhs_s[rs, :]
        gct_t = gct_s[t]
        u = jnp.zeros((TILE, GROUP_W), F32)
        w = jnp.zeros((TILE, GROUP_W), F32)
        attn = []
        for h in range(N_HEADS):
            col = gcx_s[rs, h * LANES:(h + 1) * LANES]
            row = gct_t[h:h + 1, :]
            dec = jnp.exp(jnp.where(incl, col - row, -jnp.inf))
            a_h = jnp.where(strict, _dot_nt(kbt * hm[h], kt_) * dec, 0.0)
            attn.append(_dot_nt(qt * hm[h], kt_) * dec)
            t_h = _unit_lower_inverse(-a_h, eye)
            sol = _dot(t_h, rhs)
            u = u + hm[h] * sol[:, 0:GROUP_W]
            w = w + hm[h] * sol[:, GROUP_W:2 * GROUP_W]
        qd, ktl, cd = qd_s[rs, :], kt_s[rs, :], cd_s[rs, :]
        vn, qs = [], []
        for c in range(2):
            cs = slice(c * CHUNK, (c + 1) * CHUNK)
            s = s_ref[...]
            vn_c = u[cs] - _dot(w[cs], s)
            qs.append(_dot(qd[cs], s))
            s_ref[...] = s * cd[c * CHUNK:c * CHUNK + 1, :] + bd * _dot_tn(ktl[cs], vn_c)
            vn.append(vn_c)
        vn = jnp.concatenate(vn, axis=0)
        o = jnp.concatenate(qs, axis=0)
        for h in range(N_HEADS):
            o = o + hm[h] * _dot(attn[h], vn)
        o = o * lax.rsqrt(_seg_sum(o * o, seg) * (1.0 / HEAD_DIM) + NORM_EPS) * ng
        o_ref[rs, :] = (o * _silu(p_ref[rs, 3 * GROUP_W:4 * GROUP_W])).astype(o_ref.dtype)
        return carry

    del z_all
    lax.fori_loop(0, rows // TILE, tile_body, 0)


def _chunk_mats(rows):
    ri = jnp.arange(rows)[:, None]
    ci = jnp.arange(rows)[None, :]
    same = (ri // CHUNK) == (ci // CHUNK)
    tri = (same & (ci <= ri)).astype(BF16)
    return tri, tri.T, same.astype(BF16)


def _expand_mats():
    lane = jnp.arange(LANES)[:, None]
    col = jnp.arange(GROUP_W)[None, :]
    eg = (lane == col // HEAD_DIM).astype(BF16)
    eb = (lane == N_HEADS + col // HEAD_DIM).astype(BF16)
    colx = jnp.arange(N_HEADS * LANES)[None, :]
    egx = (lane == colx // LANES).astype(BF16)
    seg = ((jnp.arange(GROUP_W)[:, None] // HEAD_DIM) == (col // HEAD_DIM)).astype(BF16)
    return eg, eb, egx, seg


def _pad_lanes(v, n=LANES):
    return jnp.zeros((1, n), F32).at[0, :v.shape[0]].set(v.astype(F32))


def _gdn(pg, pab, abt, conv_w, a_log, dt_bias, norm_g, bn, seq):
    t = pg.shape[0]
    rows = ROW_BLOCK
    nb = seq // rows
    eg, eb, egx, seg = _expand_mats()
    tri, trit, full = _chunk_mats(rows)
    alog_t = jnp.zeros((8, rows), F32).at[:N_HEADS].set(jnp.broadcast_to(a_log[:, None], (N_HEADS, rows)))
    dtb_t = jnp.zeros((8, rows), F32).at[:N_HEADS].set(jnp.broadcast_to(dt_bias[:, None], (N_HEADS, rows)))
    consts = (conv_w.astype(F32), _pad_lanes(a_log), _pad_lanes(dt_bias), alog_t, dtb_t,
              jnp.tile(norm_g.astype(F32), N_HEADS)[None, :], eg, eb, egx, seg, tri, trit, full)
    rowmap = lambda b, j: (b * nb + j, 0)
    return pl.pallas_call(
        _gdn_kernel,
        out_shape=jax.ShapeDtypeStruct((t, GROUP_W), BF16),
        grid=(bn, nb),
        in_specs=[pl.BlockSpec((rows, 4 * GROUP_W), rowmap),
                  pl.BlockSpec((rows, LANES), rowmap),
                  pl.BlockSpec((8, rows), lambda b, j: (0, b * nb + j))]
                 + [pl.BlockSpec(c.shape, lambda b, j: (0, 0)) for c in consts],
        out_specs=pl.BlockSpec((rows, GROUP_W), rowmap),
        scratch_shapes=[pltpu.VMEM((rows + 8, 3 * GROUP_W), F32),
                        pltpu.VMEM((GROUP_W, GROUP_W), F32),
                        pltpu.VMEM((rows, GROUP_W), F32), pltpu.VMEM((rows, GROUP_W), F32),
                        pltpu.VMEM((rows, GROUP_W), F32), pltpu.VMEM((rows, 2 * GROUP_W), F32),
                        pltpu.VMEM((rows, GROUP_W), F32), pltpu.VMEM((rows, GROUP_W), F32),
                        pltpu.VMEM((rows, GROUP_W), F32), pltpu.VMEM((rows, N_HEADS * LANES), F32),
                        pltpu.VMEM((rows // TILE, 8, TILE), F32)],
        compiler_params=_cparams(("arbitrary", "arbitrary")),
        name="gdn_mixer",
    )(pg, pab, abt, *consts)


def _sgu_conv_kernel(p_ref, lng_ref, lnb_ref, ws_ref, bs_ref, cw_ref, o_ref, xbuf):
    rows = p_ref.shape[0]
    j = pl.program_id(1)

    @pl.when(j == 0)
    def _():
        xbuf[0:8, :] = jnp.zeros((8, GROUP_W), F32)

    u = jax.nn.gelu(p_ref[:, 0:GROUP_W])
    vf = jax.nn.gelu(p_ref[:, GROUP_W:2 * GROUP_W])
    mean = jnp.mean(vf, -1, keepdims=True)
    var = jnp.mean(jnp.square(vf - mean), -1, keepdims=True)
    v = (vf - mean) * lax.rsqrt(var + SGU_LN_EPS) * lng_ref[...] + lnb_ref[...]
    hm = _head_masks()
    ri = lax.broadcasted_iota(jnp.int32, (SGU_CHUNK, SGU_CHUNK), 0)
    ci = lax.broadcasted_iota(jnp.int32, (SGU_CHUNK, SGU_CHUNK), 1)
    ws = [jnp.where(ri >= ci, ws_ref[h], 0.0).astype(BF16) for h in range(N_HEADS)]
    bs = bs_ref[...]
    for c in range(rows // SGU_CHUNK):
        cs = slice(c * SGU_CHUNK, (c + 1) * SGU_CHUNK)
        vc = v[cs].astype(BF16)
        mixed = bs
        for h in range(N_HEADS):
            mixed = mixed + hm[h] * jnp.dot(ws[h], vc, preferred_element_type=F32)
        o_ref[cs, 0:GROUP_W] = (u[cs] * mixed).astype(o_ref.dtype)

    xbuf[8:8 + rows, :] = p_ref[:, 3 * GROUP_W:4 * GROUP_W] * p_ref[:, 4 * GROUP_W:5 * GROUP_W]
    acc = cw_ref[2:3, :] * xbuf[8:8 + rows, :]
    for tap in range(2):
        acc = acc + cw_ref[tap:tap + 1, :] * xbuf[6 + tap:6 + tap + rows, :]
    xbuf[0:8, :] = xbuf[rows:rows + 8, :]
    o_ref[:, GROUP_W:2 * GROUP_W] = (p_ref[:, 2 * GROUP_W:3 * GROUP_W] * acc).astype(o_ref.dtype)


def _sgu_conv(psc, ln_g, ln_b, w_s, b_s, conv_w, bn, seq):
    t = psc.shape[0]
    rows = ROW_BLOCK
    nb = seq // rows
    bs_exp = jnp.repeat(b_s.T.astype(F32), HEAD_DIM, axis=1)
    consts = (ln_g[None, :].astype(F32), ln_b[None, :].astype(F32), w_s.astype(F32), bs_exp, conv_w.astype(F32))
    rowmap = lambda b, j: (b * nb + j, 0)
    return pl.pallas_call(
        _sgu_conv_kernel,
        out_shape=jax.ShapeDtypeStruct((t, 2 * GROUP_W), BF16),
        grid=(bn, nb),
        in_specs=[pl.BlockSpec((rows, 5 * GROUP_W), rowmap)]
                 + [pl.BlockSpec(c.shape, lambda b, j, n=c.ndim: (0,) * n) for c in consts],
        out_specs=pl.BlockSpec((rows, 2 * GROUP_W), rowmap),
        scratch_shapes=[pltpu.VMEM((rows + 8, GROUP_W), F32)],
        compiler_params=_cparams(("arbitrary", "arbitrary")),
        name="sgu_conv_mixer",
    )(psc, *consts)


def _rwkv_kernel(p_ref, mu_ref, w0_ref, wup_ref, a0_ref, aup_ref, gup_ref, kk_ref, ka_ref, rk_ref, gng_ref, gnb_ref,
                 seg_ref, tri_ref, full_ref,
                 o_ref,
                 prev, s_ref, at_s, bt_s, kt_s, rt_s, v_s, btl_s, ktl_s, gam_s, bon_s, gate_s):
    rows = p_ref.shape[0]
    j = pl.program_id(1)

    @pl.when(j == 0)
    def _():
        prev[...] = jnp.zeros(prev.shape, F32)
        s_ref[...] = jnp.zeros(s_ref.shape, F32)

    prev[8:8 + rows, :] = p_ref[...]
    p = p_ref[...]
    p = p + (prev[7:7 + rows, :] - p) * mu_ref[...]
    prev[0:8, :] = prev[rows:rows + 8, :]
    g_w = GROUP_W
    r = p[:, 0:g_w]
    k = p[:, g_w:2 * g_w]
    v = p[:, 2 * g_w:3 * g_w]
    o = 3 * g_w
    xw = p[:, o:o + RW_LORA_W]
    xa = p[:, o + RW_LORA_W:o + RW_LORA_W + RW_LORA_A]
    xg = p[:, o + RW_LORA_W + RW_LORA_A:o + RW_LORA_W + RW_LORA_A + RW_LORA_G]
    w_log = -_softplus(-(w0_ref[...] + _dot(jnp.tanh(xw), wup_ref[...]))) - 0.5
    lw = -jnp.exp(w_log)
    a = _sigmoid(a0_ref[...] + _dot(xa, aup_ref[...]))
    gate_s[...] = _dot(_sigmoid(xg), gup_ref[...])
    seg = seg_ref[...]
    kk = k * kk_ref[...]
    kk = kk * lax.rsqrt(_seg_sum(kk * kk, seg) + 1e-12)
    k_mod = k * (1.0 + (a - 1.0) * ka_ref[...])
    bon_s[...] = _seg_sum(r * k_mod * rk_ref[...], seg) * v
    cl = _dot_exact_x(tri_ref[...], lw)
    ct = _dot_exact_x(full_ref[...], lw)
    e_neg = jnp.exp(-cl)
    e_tail = jnp.exp(ct - cl)
    zb = kk * a
    at_s[...] = -kk * jnp.exp(cl - lw)
    bt_s[...] = zb * e_neg
    kt_s[...] = k_mod * e_neg
    rt_s[...] = r * jnp.exp(cl)
    v_s[...] = v
    btl_s[...] = zb * e_tail
    ktl_s[...] = k_mod * e_tail
    gam_s[...] = jnp.exp(ct)

    hm = _head_masks()
    strict, incl, eye = _tile_masks()
    bd = _block_diag_mask()
    gng, gnb = gng_ref[...], gnb_ref[...]

    def tile_body(t, carry):
        r0 = pl.multiple_of(t * TILE, TILE)
        rs = pl.ds(r0, TILE)
        at, bt, kt, rt, vt = at_s[rs, :], bt_s[rs, :], kt_s[rs, :], rt_s[rs, :], v_s[rs, :]
        rhs_nt = jnp.concatenate([bt, kt], axis=0).astype(BF16)
        wa = jnp.zeros((TILE, GROUP_W), F32)
        u0 = jnp.zeros((TILE, GROUP_W), F32)
        y0 = jnp.zeros((TILE, GROUP_W), F32)
        mrb = []
        for h in range(N_HEADS):
            lhs = jnp.concatenate([at * hm[h], rt * hm[h]], axis=0)
            sc = _dot_nt(lhs, rhs_nt)
            n_h = jnp.where(strict, sc[0:TILE, 0:TILE], 0.0)
            lak = jnp.where(strict, sc[0:TILE, TILE:2 * TILE], 0.0)
            mrb.append(jnp.where(incl, sc[TILE:2 * TILE, 0:TILE], 0.0))
            mrk = jnp.where(incl, sc[TILE:2 * TILE, TILE:2 * TILE], 0.0)
            t_h = _unit_lower_inverse(n_h, eye)
            sol = _dot(t_h, jnp.concatenate([at, _dot(lak, vt)], axis=1))
            wa = wa + hm[h] * sol[:, 0:GROUP_W]
            u0 = u0 + hm[h] * sol[:, GROUP_W:2 * GROUP_W]
            y0 = y0 + hm[h] * _dot(mrk, vt)
        btl, ktl, gam = btl_s[rs, :], ktl_s[rs, :], gam_s[rs, :]
        us, ys = [], []
        for c in range(2):
            cs = slice(c * CHUNK, (c + 1) * CHUNK)
            s = s_ref[...]
            u_c = _dot_nt(wa[cs], s) + u0[cs]
            ys.append(_dot_nt(rt[cs], s))
            upd = _dot_tn(jnp.concatenate([u_c, vt[cs]], axis=0), jnp.concatenate([btl[cs], ktl[cs]], axis=0))
            s_ref[...] = s * gam[c * CHUNK:c * CHUNK + 1, :] + bd * upd
            us.append(u_c)
        u = jnp.concatenate(us, axis=0)
        y = jnp.concatenate(ys, axis=0) + y0
        for h in range(N_HEADS):
            y = y + hm[h] * _dot(mrb[h], u)
        mean = _seg_sum(y, seg) * (1.0 / HEAD_DIM)
        yc = y - mean
        var = _seg_sum(yc * yc, seg) * (1.0 / HEAD_DIM)
        yn = yc * lax.rsqrt(var + RW_GN_EPS) * gng + gnb
        o_ref[rs, :] = ((yn + bon_s[rs, :]) * gate_s[rs, :]).astype(o_ref.dtype)
        return carry

    lax.fori_loop(0, rows // TILE, tile_body, 0)


def _rwkv(prw, mu, w0, w_up, a0, a_up, g_up, k_k, k_a, r_k, gn_g, gn_b, bn, seq):
    t = prw.shape[0]
    rows = ROW_BLOCK
    nb = seq // rows
    _, _, _, seg = _expand_mats()
    tri, _, full = _chunk_mats(rows)
    row = lambda x: x.reshape(1, -1).astype(F32)
    consts = (row(mu), row(w0), w_up.astype(BF16), row(a0), a_up.astype(BF16), g_up.astype(BF16),
              row(k_k), row(k_a), row(r_k), row(gn_g), row(gn_b), seg, tri, full)
    rowmap = lambda b, j: (b * nb + j, 0)
    big = lambda: pltpu.VMEM((rows, GROUP_W), F32)
    return pl.pallas_call(
        _rwkv_kernel,
        out_shape=jax.ShapeDtypeStruct((t, GROUP_W), BF16),
        grid=(bn, nb),
        in_specs=[pl.BlockSpec((rows, 4 * GROUP_W), rowmap)]
                 + [pl.BlockSpec(c.shape, lambda b, j: (0, 0)) for c in consts],
        out_specs=pl.BlockSpec((rows, GROUP_W), rowmap),
        scratch_shapes=[pltpu.VMEM((rows + 8, 4 * GROUP_W), F32), pltpu.VMEM((GROUP_W, GROUP_W), F32)]
                       + [big() for _ in range(10)],
        compiler_params=_cparams(("arbitrary", "arbitrary")),
        name="rwkv7_mixer",
    )(prw, *consts)


def _out_router_kernel(x_ref, oa_ref, obc_ref, od_ref, wo_ref, gt_ref, g_ref, sh_ref, sc_ref,
                       wrh_ref, wrl_ref, br_ref, tri_ref,
                       xo_ref, hf_ref, route_ref, cnt_ref, carry):
    i = pl.program_id(0)

    @pl.when(i == 0)
    def _():
        carry[...] = jnp.zeros(carry.shape, F32)

    g_w = GROUP_W
    mixed = jnp.dot(oa_ref[...], wo_ref[0:g_w, :], preferred_element_type=F32)
    mixed += jnp.dot(obc_ref[...], wo_ref[g_w:3 * g_w, :], preferred_element_type=F32)
    mixed += jnp.dot(od_ref[...], wo_ref[3 * g_w:4 * g_w, :], preferred_element_type=F32)
    x = x_ref[...] + gt_ref[0] * mixed
    xo_ref[...] = x
    y = x * lax.rsqrt(jnp.mean(x * x, -1, keepdims=True) + NORM_EPS) * g_ref[...]
    hf = y * (1.0 + sc_ref[0]) + sh_ref[0]
    hf_ref[...] = hf
    hh, hl = _split2(hf)
    lg = (jnp.dot(hh, wrh_ref[...], preferred_element_type=F32) + jnp.dot(hl, wrh_ref[...], preferred_element_type=F32)
          + jnp.dot(hh, wrl_ref[...], preferred_element_type=F32) + br_ref[...])
    tm = lg.shape[0]
    lane = lax.broadcasted_iota(jnp.int32, (tm, LANES), 1)
    lanef = lane.astype(F32)
    big = jnp.float32(1e9)
    ninf = jnp.float32(-jnp.inf)
    is_g = (lane >= N_EXPERTS) & (lane < N_EXPERTS + N_GROUPS)
    gl = jnp.where(is_g, lg, ninf)
    gmax = jnp.max(gl, -1, keepdims=True)
    gsel = jnp.min(jnp.where(gl == gmax, lanef - N_EXPERTS, big), -1, keepdims=True)
    p_group = 1.0 / jnp.sum(jnp.where(is_g, jnp.exp(gl - gmax), 0.0), -1, keepdims=True)
    in_grp = (lane < N_EXPERTS) & ((lane >> 3).astype(F32) == gsel)
    el = jnp.where(in_grp, lg, ninf)
    v1 = jnp.max(el, -1, keepdims=True)
    i1 = jnp.min(jnp.where(el == v1, lanef, big), -1, keepdims=True)
    el2 = jnp.where(lanef == i1, ninf, el)
    v2 = jnp.max(el2, -1, keepdims=True)
    i2 = jnp.min(jnp.where(el2 == v2, lanef, big), -1, keepdims=True)
    e21 = jnp.exp(v2 - v1)
    g1 = p_group / (1.0 + e21)
    g2 = p_group * e21 / (1.0 + e21)
    oh1 = lanef == i1
    oh2 = lanef == i2
    oh = jnp.where(oh1 | oh2, 1.0, 0.0)
    total = jnp.dot(tri_ref[...], oh.astype(BF16), preferred_element_type=F32) + carry[...]
    rank1 = jnp.sum(jnp.where(oh1, total, 0.0), -1, keepdims=True)
    rank2 = jnp.sum(jnp.where(oh2, total, 0.0), -1, keepdims=True)
    carry[...] = carry[...] + jnp.sum(oh, axis=0, keepdims=True)
    cnt_ref[...] = carry[...]
    route = jnp.where(lane == 0, i1, jnp.where(lane == 1, i2, jnp.where(lane == 2, rank1, jnp.where(
        lane == 3, rank2, jnp.where(lane == 4, g1, jnp.where(lane == 5, g2, 0.0))))))
    route_ref[...] = route


def _out_router(x, oa, obc, od, w_out, gt, g, shift, scale, wr_hi, wr_lo, b_r, seq):
    t, d = x.shape
    tm = ROW_BLOCK
    per_b = seq // tm
    tri = (jnp.arange(tm)[:, None] > jnp.arange(tm)[None, :]).astype(BF16)
    bspec = pl.BlockSpec((1, 1, d), lambda i: (i // per_b, 0, 0))
    full = lambda a: pl.BlockSpec(a.shape, lambda i: (0,) * a.ndim)
    return pl.pallas_call(
        _out_router_kernel,
        out_shape=(jax.ShapeDtypeStruct((t, d), F32), jax.ShapeDtypeStruct((t, d), F32),
                   jax.ShapeDtypeStruct((t, LANES), F32), jax.ShapeDtypeStruct((1, LANES), F32)),
        grid=(t // tm,),
        in_specs=[pl.BlockSpec((tm, d), lambda i: (i, 0)),
                  pl.BlockSpec((tm, GROUP_W), lambda i: (i, 0)),
                  pl.BlockSpec((tm, 2 * GROUP_W), lambda i: (i, 0)),
                  pl.BlockSpec((tm, GROUP_W), lambda i: (i, 0)),
                  full(w_out), bspec, full(g), bspec, bspec, full(wr_hi), full(wr_lo), full(b_r), full(tri)],
        out_specs=(pl.BlockSpec((tm, d), lambda i: (i, 0)), pl.BlockSpec((tm, d), lambda i: (i, 0)),
                   pl.BlockSpec((tm, LANES), lambda i: (i, 0)), pl.BlockSpec((1, LANES), lambda i: (0, 0))),
        scratch_shapes=[pltpu.VMEM((1, LANES), F32)],
        compiler_params=_cparams(("arbitrary",)),
        name="out_proj_router",
    )(x, oa, obc, od, w_out, gt, g, shift, scale, wr_hi, wr_lo, b_r, tri)


def _row_copy(src, src_row, dst, dst_row, sem):
    return pltpu.make_async_copy(src.at[pl.ds(src_row, 1), :], dst.at[pl.ds(dst_row, 1), :], sem)


def _dispatch_kernel(dest_ref, hf_ref, xs_in_ref, xs_ref, sem):
    del xs_in_ref
    n_tok = hf_ref.shape[0]

    def issue(r, carry):
        _row_copy(hf_ref, r, xs_ref, dest_ref[0, 0, 2 * r], sem).start()
        _row_copy(hf_ref, r, xs_ref, dest_ref[0, 0, 2 * r + 1], sem).start()
        return carry

    lax.fori_loop(0, n_tok, issue, 0)

    def drain(r, carry):
        _row_copy(hf_ref, 0, xs_ref, 0, sem).wait()
        return carry

    lax.fori_loop(0, 2 * n_tok, drain, 0)


def _dispatch(dest3, hf, p_rows):
    t, d = hf.shape
    td = MOE_TOK
    xs0 = jnp.zeros((p_rows, d), F32)
    return pl.pallas_call(
        _dispatch_kernel,
        out_shape=jax.ShapeDtypeStruct((p_rows, d), F32),
        grid=(t // td,),
        in_specs=[pl.BlockSpec((1, 1, 2 * td), lambda i: (i, 0, 0), memory_space=pltpu.SMEM),
                  pl.BlockSpec((td, d), lambda i: (i, 0)),
                  pl.BlockSpec(memory_space=pl.ANY)],
        out_specs=pl.BlockSpec(memory_space=pl.ANY),
        scratch_shapes=[pltpu.SemaphoreType.DMA(())],
        input_output_aliases={2: 0},
        compiler_params=_cparams(("arbitrary",)),
        name="moe_dispatch",
    )(dest3, hf, xs0)


def _expert_kernel(be_ref, nu_ref, xs_ref, wg_ref, wu_ref, wd_ref, ys_ref):
    i = pl.program_id(0)

    @pl.when(i < nu_ref[0])
    def _():
        xb = xs_ref[...].astype(BF16)
        hid = _silu(jnp.dot(xb, wg_ref[...], preferred_element_type=F32)) * jnp.dot(xb, wu_ref[...], preferred_element_type=F32)
        ys_ref[...] = jnp.dot(hid.astype(BF16), wd_ref[...], preferred_element_type=F32)

    @pl.when(i >= nu_ref[0])
    def _():
        ys_ref[...] = jnp.zeros(ys_ref.shape, F32)


def _experts(blk_e, n_used, xs, w_gate, w_up, w_down):
    p_rows, d = xs.shape
    bm = MOE_ROWS
    grid_spec = pltpu.PrefetchScalarGridSpec(
        num_scalar_prefetch=2,
        grid=(p_rows // bm,),
        in_specs=[pl.BlockSpec((bm, d), lambda i, be, nu: (i, 0)),
                  pl.BlockSpec((None, d, D_EXPERT), lambda i, be, nu: (be[i], 0, 0)),
                  pl.BlockSpec((None, d, D_EXPERT), lambda i, be, nu: (be[i], 0, 0)),
                  pl.BlockSpec((None, D_EXPERT, d), lambda i, be, nu: (be[i], 0, 0))],
        out_specs=pl.BlockSpec((bm, d), lambda i, be, nu: (i, 0)),
    )
    return pl.pallas_call(
        _expert_kernel,
        out_shape=jax.ShapeDtypeStruct((p_rows, d), F32),
        grid_spec=grid_spec,
        compiler_params=_cparams(("arbitrary",)),
        name="moe_experts",
    )(blk_e, n_used, xs, w_gate, w_up, w_down)


def _combine_kernel(dest_ref, route_ref, x_ref, gt_ref, fg_ref, ys_ref, o_ref, y1, y2, sem, *, final):
    n_tok = x_ref.shape[0]

    def issue(r, carry):
        _row_copy(ys_ref, dest_ref[0, 0, 2 * r], y1, r, sem).start()
        _row_copy(ys_ref, dest_ref[0, 0, 2 * r + 1], y2, r, sem).start()
        return carry

    lax.fori_loop(0, n_tok, issue, 0)

    def drain(r, carry):
        _row_copy(ys_ref, 0, y1, 0, sem).wait()
        return carry

    lax.fori_loop(0, 2 * n_tok, drain, 0)
    route = route_ref[...]
    y = route[:, 4:5] * y1[...] + route[:, 5:6] * y2[...]
    x = x_ref[...] + gt_ref[0] * y
    if final:
        x = x * lax.rsqrt(jnp.mean(x * x, -1, keepdims=True) + NORM_EPS) * fg_ref[...]
    o_ref[...] = x


def _combine(dest3, route, x, gt, final_g, ys, seq, final):
    t, d = x.shape
    tc = MOE_TOK
    per_b = seq // tc
    return pl.pallas_call(
        functools.partial(_combine_kernel, final=final),
        out_shape=jax.ShapeDtypeStruct((t, d), F32),
        grid=(t // tc,),
        in_specs=[pl.BlockSpec((1, 1, 2 * tc), lambda i: (i, 0, 0), memory_space=pltpu.SMEM),
                  pl.BlockSpec((tc, LANES), lambda i: (i, 0)),
                  pl.BlockSpec((tc, d), lambda i: (i, 0)),
                  pl.BlockSpec((1, 1, d), lambda i: (i // per_b, 0, 0)),
                  pl.BlockSpec((1, d), lambda i: (0, 0)),
                  pl.BlockSpec(memory_space=pl.ANY)],
        out_specs=pl.BlockSpec((tc, d), lambda i: (i, 0)),
        scratch_shapes=[pltpu.VMEM((tc, d), F32), pltpu.VMEM((tc, d), F32), pltpu.SemaphoreType.DMA(())],
        compiler_params=_cparams(("arbitrary",)),
        name="moe_combine",
    )(dest3, route, x, gt, final_g, ys)


def _route_plan(route, counts, t):
    bm = MOE_ROWS
    cnt = counts[0, :N_EXPERTS].astype(jnp.int32)
    padded = (cnt + bm - 1) // bm * bm
    pad_end = jnp.cumsum(padded)
    pad_start = pad_end - padded
    e = route[:, 0:2].astype(jnp.int32)
    rank = route[:, 2:4].astype(jnp.int32)
    dest = pad_start[e] + rank
    p_rows = 2 * t + N_EXPERTS * bm
    n_blk = p_rows // bm
    blk_e = jnp.minimum(jnp.searchsorted(pad_end, jnp.arange(n_blk, dtype=jnp.int32) * bm, side='right'),
                        N_EXPERTS - 1).astype(jnp.int32)
    n_used = (pad_end[-1:] // bm).astype(jnp.int32)
    dest3 = dest.reshape(t // MOE_TOK, 1, 2 * MOE_TOK)
    return dest3, blk_e, n_used, p_rows


def kernel(x, c, ada_w, ada_b, mix_norm_g, ffn_norm_g, w_in, w_out, gdn_conv_w, gdn_a_log, gdn_dt_bias, gdn_norm_g,
           sgu_ln_g, sgu_ln_b, sgu_w, sgu_b, sc_conv_w, rw_mu, rw_w0, rw_w_up, rw_a0, rw_a_up, rw_g_up, rw_k_k,
           rw_k_a, rw_r_k, rw_gn_g, rw_gn_b, moe_w_group, moe_b_group, moe_w_router, moe_b_router, moe_w_gate,
           moe_w_up, moe_w_down, final_norm_g):
    bn, seq, d = x.shape
    depth = ada_w.shape[0]
    t = bn * seq
    assert d == 4 * GROUP_W and seq % ROW_BLOCK == 0 and ROW_BLOCK % MOE_TOK == 0
    g_w = GROUP_W
    mod = _ada(c, ada_w, ada_b)
    xf = x.reshape(t, d)
    o_z, o_a, o_su = 3 * g_w, 4 * g_w, 4 * g_w + 2 * N_HEADS
    o_rp = o_su + 5 * g_w
    for l in range(depth):
        m = mod[l].reshape(bn, 6, 1, d)
        sh_m, sc_m, gt_m, sh_f, sc_f, gt_f = (m[:, i] for i in range(6))
        wl = w_in[l]
        w_ab = wl[:, o_a:o_su]
        w_r = jnp.concatenate([wl[:, 0:o_a], wl[:, o_su:o_rp], wl[:, o_rp:],
                               jnp.pad(w_ab, ((0, 0), (0, LANES - 2 * N_HEADS)))], axis=1).astype(BF16)
        pg, psc, prw, pab, abt = _in_proj(xf, mix_norm_g[l][None, :], sh_m, sc_m, w_r, w_ab.T.astype(BF16), seq)
        oa = _gdn(pg, pab, abt, gdn_conv_w[l], gdn_a_log[l], gdn_dt_bias[l], gdn_norm_g[l], bn, seq)
        obc = _sgu_conv(psc, sgu_ln_g[l], sgu_ln_b[l], sgu_w[l], sgu_b[l], sc_conv_w[l], bn, seq)
        od = _rwkv(prw, rw_mu[l], rw_w0[l], rw_w_up[l], rw_a0[l], rw_a_up[l], rw_g_up[l], rw_k_k[l], rw_k_a[l],
                   rw_r_k[l], rw_gn_g[l], rw_gn_b[l], bn, seq)
        w_rt = jnp.concatenate([moe_w_router[l], moe_w_group[l],
                                jnp.zeros((d, LANES - N_EXPERTS - N_GROUPS), F32)], axis=1)
        wr_hi = w_rt.astype(BF16)
        wr_lo = (w_rt - wr_hi.astype(F32)).astype(BF16)
        b_r = jnp.concatenate([moe_b_router[l], moe_b_group[l], jnp.zeros((LANES - N_EXPERTS - N_GROUPS,), F32)])[None, :]
        xf, hf, route, counts = _out_router(xf, oa, obc, od, w_out[l].astype(BF16), gt_m, ffn_norm_g[l][None, :],
                                            sh_f, sc_f, wr_hi, wr_lo, b_r, seq)
        dest3, blk_e, n_used, p_rows = _route_plan(route, counts, t)
        xs = _dispatch(dest3, hf, p_rows)
        ys = _experts(blk_e, n_used, xs, moe_w_gate[l].astype(BF16), moe_w_up[l].astype(BF16), moe_w_down[l].astype(BF16))
        xf = _combine(dest3, route, xf, gt_f, final_norm_g[None, :], ys, seq, final=(l == depth - 1))
    return xf.reshape(bn, seq, d)
```

```python
import functools
import math

import jax
import jax.numpy as jnp
from jax import lax
from jax.experimental import pallas as pl
from jax.experimental.pallas import tpu as pltpu

F32 = jnp.float32
BF16 = jnp.bfloat16

HEAD_DIM = 64
N_HEADS = 4
GROUP_W = HEAD_DIM * N_HEADS
CHUNK = 64
TILE = 2 * CHUNK
NORM_EPS = 1e-6
SGU_CHUNK = 128
SGU_LN_EPS = 1e-5
RW_GN_EPS = 64e-5
RW_LORA_W, RW_LORA_A, RW_LORA_G = 64, 64, 128
N_GROUPS = 8
EXPERTS_PER_GROUP = 8
N_EXPERTS = N_GROUPS * EXPERTS_PER_GROUP
D_EXPERT = 256
LANES = 128
VMEM_LIMIT = 56 * 1024 * 1024

ROW_BLOCK = 512
MOE_ROWS = 256
MOE_TOK = 256


def _dot(a, b):
    return jnp.dot(a.astype(BF16), b.astype(BF16), preferred_element_type=F32)


def _dot_nt(a, b):
    return lax.dot_general(a.astype(BF16), b.astype(BF16), (((1,), (1,)), ((), ())),
                           preferred_element_type=F32)


def _dot_tn(a, b):
    return lax.dot_general(a.astype(BF16), b.astype(BF16), (((0,), (0,)), ((), ())),
                           preferred_element_type=F32)


def _split2(x):
    hi = x.astype(BF16)
    lo = (x - hi.astype(F32)).astype(BF16)
    return hi, lo


def _dot_x_exact(x, m):
    hi, lo = _split2(x)
    return (jnp.dot(hi, m, preferred_element_type=F32) + jnp.dot(lo, m, preferred_element_type=F32))


def _dot_exact_x(m, x):
    hi, lo = _split2(x)
    return (jnp.dot(m, hi, preferred_element_type=F32) + jnp.dot(m, lo, preferred_element_type=F32))


def _sigmoid(x):
    return 1.0 / (1.0 + jnp.exp(-x))


def _silu(x):
    return x * _sigmoid(x)


def _softplus(x):
    return jnp.maximum(x, 0.0) + jnp.log(1.0 + jnp.exp(-jnp.abs(x)))


def _head_masks():
    lane = lax.broadcasted_iota(jnp.int32, (1, GROUP_W), 1)
    return [((lane >> 6) == h).astype(F32) for h in range(N_HEADS)]


def _tile_masks():
    ri = lax.broadcasted_iota(jnp.int32, (TILE, TILE), 0)
    ci = lax.broadcasted_iota(jnp.int32, (TILE, TILE), 1)
    same = (ri >> 6) == (ci >> 6)
    return same & (ri > ci), same & (ri >= ci), (ri == ci).astype(F32)


def _block_diag_mask():
    ri = lax.broadcasted_iota(jnp.int32, (GROUP_W, GROUP_W), 0)
    ci = lax.broadcasted_iota(jnp.int32, (GROUP_W, GROUP_W), 1)
    return ((ri >> 6) == (ci >> 6)).astype(F32)


def _merge_masks():
    ri = lax.broadcasted_iota(jnp.int32, (TILE, TILE), 0)
    ci = lax.broadcasted_iota(jnp.int32, (TILE, TILE), 1)
    return [((ri >> (l + 1)) == (ci >> (l + 1))) & (((ri >> l) & 1) == 1) & (((ci >> l) & 1) == 0) for l in range(6)]


def _unit_lower_inverses(a, eye, chains):
    masks = _merge_masks()
    d = {c: eye - jnp.where(masks[0], a[c], 0.0) for c in chains}
    for l in range(1, 6):
        f = {c: _dot(jnp.where(masks[l], a[c], 0.0), d[c]) for c in chains}
        d = {c: d[c] - _dot(d[c], f[c]) for c in chains}
    return d


def _seg_sum(x, seg):
    return _dot_x_exact(x, seg)


def _cparams(sem):
    return pltpu.CompilerParams(dimension_semantics=sem, vmem_limit_bytes=VMEM_LIMIT)


def _ada_kernel(c_ref, w_ref, b_ref, o_ref):
    c = c_ref[...]
    ca = _silu(c)
    chi, clo = _split2(ca)
    w = w_ref[...]
    whi, wlo = _split2(w)
    acc = jnp.dot(chi, whi, preferred_element_type=F32)
    acc += jnp.dot(clo, whi, preferred_element_type=F32)
    acc += jnp.dot(chi, wlo, preferred_element_type=F32)
    o_ref[...] = acc + b_ref[...]


def _ada(c, ada_w, ada_b):
    depth, d, d6 = ada_w.shape
    bn = c.shape[0]
    nj = d6 // d
    return pl.pallas_call(
        _ada_kernel,
        out_shape=jax.ShapeDtypeStruct((depth, bn, d6), F32),
        grid=(depth, nj),
        in_specs=[pl.BlockSpec((bn, d), lambda l, j: (0, 0)),
                  pl.BlockSpec((None, d, d), lambda l, j: (l, 0, j)),
                  pl.BlockSpec((None, 1, d), lambda l, j: (l, 0, j))],
        out_specs=pl.BlockSpec((None, bn, d), lambda l, j: (l, 0, j)),
        compiler_params=_cparams(("arbitrary", "arbitrary")),
        name="ada_mod",
    )(c, ada_w, ada_b.reshape(depth, 1, d6))


def _in_proj_kernel(x_ref, g_ref, sh_ref, sc_ref, w_ref, wab_ref, pg_ref, psc_ref, prw_ref, pab_ref, abt_ref):
    x = x_ref[...]
    y = x * lax.rsqrt(jnp.mean(x * x, -1, keepdims=True) + NORM_EPS) * g_ref[...]
    h = (y * (1.0 + sc_ref[0]) + sh_ref[0]).astype(BF16)
    o = 0
    for ref in (pg_ref, psc_ref, prw_ref, pab_ref):
        w = ref.shape[1]
        ref[...] = jnp.dot(h, w_ref[:, o:o + w], preferred_element_type=F32)
        o += w
    abt_ref[...] = lax.dot_general(wab_ref[...], h, (((1,), (1,)), ((), ())), preferred_element_type=F32)


def _in_proj(x, g, shift, scale, w_r, w_abt, seq):
    t, d = x.shape
    tm = ROW_BLOCK
    per_b = seq // tm
    widths = (4 * GROUP_W, 5 * GROUP_W, 4 * GROUP_W, LANES)
    bspec = pl.BlockSpec((1, 1, d), lambda i: (i // per_b, 0, 0))
    return pl.pallas_call(
        _in_proj_kernel,
        out_shape=tuple(jax.ShapeDtypeStruct((t, w), F32) for w in widths) + (jax.ShapeDtypeStruct((8, t), F32),),
        grid=(t // tm,),
        in_specs=[pl.BlockSpec((tm, d), lambda i: (i, 0)),
                  pl.BlockSpec((1, d), lambda i: (0, 0)),
                  bspec, bspec,
                  pl.BlockSpec(w_r.shape, lambda i: (0, 0)),
                  pl.BlockSpec(w_abt.shape, lambda i: (0, 0))],
        out_specs=tuple(pl.BlockSpec((tm, w), lambda i: (i, 0)) for w in widths) + (pl.BlockSpec((8, tm), lambda i: (0, i)),),
        compiler_params=_cparams(("arbitrary",)),
        name="in_proj",
    )(x, g, shift, scale, w_r, w_abt)


def _gdn_kernel(p_ref, ab_ref, abt_ref, cw_ref, alog_ref, dtb_ref, alogt_ref, dtbt_ref, ng_ref,
                eg_ref, eb_ref, egx_ref, seg_ref, tri_ref, trit_ref, full_ref,
                o_ref,
                xbuf, s_ref, q_s, k_s, kb_s, rhs_s, qd_s, kt_s, cd_s, gcx_s, gct_s, u_s, w_s):
    rows = p_ref.shape[0]
    j = pl.program_id(1)

    @pl.when(j == 0)
    def _():
        xbuf[0:8, :] = jnp.zeros((8, xbuf.shape[1]), F32)
        s_ref[...] = jnp.zeros(s_ref.shape, F32)

    xbuf[8:8 + rows, :] = p_ref[:, 0:3 * GROUP_W]
    acc = cw_ref[3:4, :] * xbuf[8:8 + rows, :]
    for tap in range(3):
        acc = acc + cw_ref[tap:tap + 1, :] * xbuf[5 + tap:5 + tap + rows, :]
    xbuf[0:8, :] = xbuf[rows:rows + 8, :]
    qkv = _silu(acc)
    seg = seg_ref[...]
    q = qkv[:, 0:GROUP_W]
    k = qkv[:, GROUP_W:2 * GROUP_W]
    v = qkv[:, 2 * GROUP_W:3 * GROUP_W]
    q = q * lax.rsqrt(_seg_sum(q * q, seg) + 1e-6) * (HEAD_DIM ** -0.5)
    k = k * lax.rsqrt(_seg_sum(k * k, seg) + 1e-6)

    ab = ab_ref[...]
    g = -jnp.exp(alog_ref[...]) * _softplus(ab + dtb_ref[...])
    beta = _dot_x_exact(_sigmoid(ab), eb_ref[...])
    gc = _dot_exact_x(tri_ref[...], _dot_x_exact(g, eg_ref[...]))
    gl = _dot_exact_x(full_ref[...], _dot_x_exact(g, eg_ref[...]))
    egc = jnp.exp(gc)
    kb = k * beta
    q_s[...] = q
    k_s[...] = k
    kb_s[...] = kb
    rhs_s[:, 0:GROUP_W] = v * beta
    rhs_s[:, GROUP_W:2 * GROUP_W] = kb * egc
    qd_s[...] = q * egc
    kt_s[...] = k * jnp.exp(gl - gc)
    cd_s[...] = jnp.exp(gl)
    gcx_s[...] = _dot_exact_x(tri_ref[...], _dot_x_exact(g, egx_ref[...]))
    abt = abt_ref[...]
    gt = -jnp.exp(alogt_ref[...]) * _softplus(abt + dtbt_ref[...])
    gct = _dot_x_exact(gt, trit_ref[...])
    for t in range(rows // TILE):
        gct_s[t] = gct[:, t * TILE:(t + 1) * TILE]

    hm = _head_masks()
    strict, incl, eye = _tile_masks()
    bd = _block_diag_mask()
    ng = ng_ref[...]
    tiles = range(rows // TILE)
    heads = range(N_HEADS)
    chains = [(t, h) for t in tiles for h in heads]
    rsl = [slice(t * TILE, (t + 1) * TILE) for t in tiles]

    x, attn = {}, {}
    for t in tiles:
        kt_ = k_s[rsl[t], :].astype(BF16)
        kbt, qt = kb_s[rsl[t], :], q_s[rsl[t], :]
        gct_t = gct_s[t]
        for h in heads:
            dec = jnp.exp(jnp.where(incl, gcx_s[rsl[t], h * LANES:(h + 1) * LANES] - gct_t[h:h + 1, :], -jnp.inf))
            x[t, h] = jnp.where(strict, _dot_nt(kbt * hm[h], kt_) * dec, 0.0)
            attn[t, h] = (_dot_nt(qt * hm[h], kt_) * dec).astype(BF16)
    p = _unit_lower_inverses(x, eye, chains)
    for t in tiles:
        rhs = rhs_s[rsl[t], :].astype(BF16)
        u = jnp.zeros((TILE, GROUP_W), F32)
        w = jnp.zeros((TILE, GROUP_W), F32)
        for h in heads:
            sol = _dot(p[t, h], rhs)
            u = u + hm[h] * sol[:, 0:GROUP_W]
            w = w + hm[h] * sol[:, GROUP_W:2 * GROUP_W]
        u_s[rsl[t], :] = u
        w_s[rsl[t], :] = w

    s = s_ref[...]
    for c in range(rows // CHUNK):
        cs = slice(c * CHUNK, (c + 1) * CHUNK)
        vn_c = u_s[cs, :] - _dot(w_s[cs, :], s)
        w_s[cs, :] = _dot(qd_s[cs, :], s)
        u_s[cs, :] = vn_c
        s = s * cd_s[c * CHUNK:c * CHUNK + 1, :] + bd * _dot_tn(kt_s[cs, :], vn_c)
    s_ref[...] = s

    for t in tiles:
        vn = u_s[rsl[t], :].astype(BF16)
        o = w_s[rsl[t], :]
        for h in heads:
            o = o + hm[h] * _dot(attn[t, h], vn)
        o = o * lax.rsqrt(_seg_sum(o * o, seg) * (1.0 / HEAD_DIM) + NORM_EPS) * ng
        o_ref[rsl[t], :] = (o * _silu(p_ref[rsl[t], 3 * GROUP_W:4 * GROUP_W])).astype(o_ref.dtype)


def _chunk_mats(rows):
    ri = jnp.arange(rows)[:, None]
    ci = jnp.arange(rows)[None, :]
    same = (ri // CHUNK) == (ci // CHUNK)
    tri = (same & (ci <= ri)).astype(BF16)
    return tri, tri.T, same.astype(BF16)


def _expand_mats():
    lane = jnp.arange(LANES)[:, None]
    col = jnp.arange(GROUP_W)[None, :]
    eg = (lane == col // HEAD_DIM).astype(BF16)
    eb = (lane == N_HEADS + col // HEAD_DIM).astype(BF16)
    colx = jnp.arange(N_HEADS * LANES)[None, :]
    egx = (lane == colx // LANES).astype(BF16)
    seg = ((jnp.arange(GROUP_W)[:, None] // HEAD_DIM) == (col // HEAD_DIM)).astype(BF16)
    return eg, eb, egx, seg


def _pad_lanes(v, n=LANES):
    return jnp.zeros((1, n), F32).at[0, :v.shape[0]].set(v.astype(F32))


def _gdn(pg, pab, abt, conv_w, a_log, dt_bias, norm_g, bn, seq):
    t = pg.shape[0]
    rows = ROW_BLOCK
    nb = seq // rows
    eg, eb, egx, seg = _expand_mats()
    tri, trit, full = _chunk_mats(rows)
    alog_t = jnp.zeros((8, rows), F32).at[:N_HEADS].set(jnp.broadcast_to(a_log[:, None], (N_HEADS, rows)))
    dtb_t = jnp.zeros((8, rows), F32).at[:N_HEADS].set(jnp.broadcast_to(dt_bias[:, None], (N_HEADS, rows)))
    consts = (conv_w.astype(F32), _pad_lanes(a_log), _pad_lanes(dt_bias), alog_t, dtb_t,
              jnp.tile(norm_g.astype(F32), N_HEADS)[None, :], eg, eb, egx, seg, tri, trit, full)
    rowmap = lambda b, j: (b * nb + j, 0)
    return pl.pallas_call(
        _gdn_kernel,
        out_shape=jax.ShapeDtypeStruct((t, GROUP_W), BF16),
        grid=(bn, nb),
        in_specs=[pl.BlockSpec((rows, 4 * GROUP_W), rowmap),
                  pl.BlockSpec((rows, LANES), rowmap),
                  pl.BlockSpec((8, rows), lambda b, j: (0, b * nb + j))]
                 + [pl.BlockSpec(c.shape, lambda b, j: (0, 0)) for c in consts],
        out_specs=pl.BlockSpec((rows, GROUP_W), rowmap),
        scratch_shapes=[pltpu.VMEM((rows + 8, 3 * GROUP_W), F32),
                        pltpu.VMEM((GROUP_W, GROUP_W), F32),
                        pltpu.VMEM((rows, GROUP_W), F32), pltpu.VMEM((rows, GROUP_W), F32),
                        pltpu.VMEM((rows, GROUP_W), F32), pltpu.VMEM((rows, 2 * GROUP_W), F32),
                        pltpu.VMEM((rows, GROUP_W), F32), pltpu.VMEM((rows, GROUP_W), F32),
                        pltpu.VMEM((rows, GROUP_W), F32), pltpu.VMEM((rows, N_HEADS * LANES), F32),
                        pltpu.VMEM((rows // TILE, 8, TILE), F32),
                        pltpu.VMEM((rows, GROUP_W), F32), pltpu.VMEM((rows, GROUP_W), F32)],
        compiler_params=_cparams(("arbitrary", "arbitrary")),
        name="gdn_mixer",
    )(pg, pab, abt, *consts)


def _sgu_conv_kernel(p_ref, lng_ref, lnb_ref, ws_ref, bs_ref, cw_ref, o_ref, xbuf):
    rows = p_ref.shape[0]
    j = pl.program_id(1)

    @pl.when(j == 0)
    def _():
        xbuf[0:8, :] = jnp.zeros((8, GROUP_W), F32)

    u = jax.nn.gelu(p_ref[:, 0:GROUP_W])
    vf = jax.nn.gelu(p_ref[:, GROUP_W:2 * GROUP_W])
    mean = jnp.mean(vf, -1, keepdims=True)
    var = jnp.mean(jnp.square(vf - mean), -1, keepdims=True)
    v = (vf - mean) * lax.rsqrt(var + SGU_LN_EPS) * lng_ref[...] + lnb_ref[...]
    hm = _head_masks()
    ri = lax.broadcasted_iota(jnp.int32, (SGU_CHUNK, SGU_CHUNK), 0)
    ci = lax.broadcasted_iota(jnp.int32, (SGU_CHUNK, SGU_CHUNK), 1)
    ws = [jnp.where(ri >= ci, ws_ref[h], 0.0).astype(BF16) for h in range(N_HEADS)]
    bs = bs_ref[...]
    for c in range(rows // SGU_CHUNK):
        cs = slice(c * SGU_CHUNK, (c + 1) * SGU_CHUNK)
        vc = v[cs].astype(BF16)
        mixed = bs
        for h in range(N_HEADS):
            mixed = mixed + hm[h] * jnp.dot(ws[h], vc, preferred_element_type=F32)
        o_ref[cs, 0:GROUP_W] = (u[cs] * mixed).astype(o_ref.dtype)

    xbuf[8:8 + rows, :] = p_ref[:, 3 * GROUP_W:4 * GROUP_W] * p_ref[:, 4 * GROUP_W:5 * GROUP_W]
    acc = cw_ref[2:3, :] * xbuf[8:8 + rows, :]
    for tap in range(2):
        acc = acc + cw_ref[tap:tap + 1, :] * xbuf[6 + tap:6 + tap + rows, :]
    xbuf[0:8, :] = xbuf[rows:rows + 8, :]
    o_ref[:, GROUP_W:2 * GROUP_W] = (p_ref[:, 2 * GROUP_W:3 * GROUP_W] * acc).astype(o_ref.dtype)


def _sgu_conv(psc, ln_g, ln_b, w_s, b_s, conv_w, bn, seq):
    t = psc.shape[0]
    rows = ROW_BLOCK
    nb = seq // rows
    bs_exp = jnp.repeat(b_s.T.astype(F32), HEAD_DIM, axis=1)
    consts = (ln_g[None, :].astype(F32), ln_b[None, :].astype(F32), w_s.astype(F32), bs_exp, conv_w.astype(F32))
    rowmap = lambda b, j: (b * nb + j, 0)
    return pl.pallas_call(
        _sgu_conv_kernel,
        out_shape=jax.ShapeDtypeStruct((t, 2 * GROUP_W), BF16),
        grid=(bn, nb),
        in_specs=[pl.BlockSpec((rows, 5 * GROUP_W), rowmap)]
                 + [pl.BlockSpec(c.shape, lambda b, j, n=c.ndim: (0,) * n) for c in consts],
        out_specs=pl.BlockSpec((rows, 2 * GROUP_W), rowmap),
        scratch_shapes=[pltpu.VMEM((rows + 8, GROUP_W), F32)],
        compiler_params=_cparams(("arbitrary", "arbitrary")),
        name="sgu_conv_mixer",
    )(psc, *consts)


def _rwkv_kernel(p_ref, mu_ref, w0_ref, wup_ref, a0_ref, aup_ref, gup_ref, kk_ref, ka_ref, rk_ref, gng_ref, gnb_ref,
                 seg_ref, tri_ref, full_ref,
                 o_ref,
                 prev, s_ref, at_s, bt_s, kt_s, rt_s, v_s, btl_s, ktl_s, gam_s, bon_s, gate_s, wa_s, u_s, y_s):
    rows = p_ref.shape[0]
    j = pl.program_id(1)

    @pl.when(j == 0)
    def _():
        prev[...] = jnp.zeros(prev.shape, F32)
        s_ref[...] = jnp.zeros(s_ref.shape, F32)

    prev[8:8 + rows, :] = p_ref[...]
    p = p_ref[...]
    p = p + (prev[7:7 + rows, :] - p) * mu_ref[...]
    prev[0:8, :] = prev[rows:rows + 8, :]
    g_w = GROUP_W
    r = p[:, 0:g_w]
    k = p[:, g_w:2 * g_w]
    v = p[:, 2 * g_w:3 * g_w]
    o = 3 * g_w
    xw = p[:, o:o + RW_LORA_W]
    xa = p[:, o + RW_LORA_W:o + RW_LORA_W + RW_LORA_A]
    xg = p[:, o + RW_LORA_W + RW_LORA_A:o + RW_LORA_W + RW_LORA_A + RW_LORA_G]
    w_log = -_softplus(-(w0_ref[...] + _dot(jnp.tanh(xw), wup_ref[...]))) - 0.5
    lw = -jnp.exp(w_log)
    a = _sigmoid(a0_ref[...] + _dot(xa, aup_ref[...]))
    gate_s[...] = _dot(_sigmoid(xg), gup_ref[...])
    seg = seg_ref[...]
    kk = k * kk_ref[...]
    kk = kk * lax.rsqrt(_seg_sum(kk * kk, seg) + 1e-12)
    k_mod = k * (1.0 + (a - 1.0) * ka_ref[...])
    bon_s[...] = _seg_sum(r * k_mod * rk_ref[...], seg) * v
    cl = _dot_exact_x(tri_ref[...], lw)
    ct = _dot_exact_x(full_ref[...], lw)
    e_neg = jnp.exp(-cl)
    e_tail = jnp.exp(ct - cl)
    zb = kk * a
    at_s[...] = -kk * jnp.exp(cl - lw)
    bt_s[...] = zb * e_neg
    kt_s[...] = k_mod * e_neg
    rt_s[...] = r * jnp.exp(cl)
    v_s[...] = v
    btl_s[...] = zb * e_tail
    ktl_s[...] = k_mod * e_tail
    gam_s[...] = jnp.exp(ct)

    hm = _head_masks()
    strict, incl, eye = _tile_masks()
    bd = _block_diag_mask()
    gng, gnb = gng_ref[...], gnb_ref[...]

    tiles = range(rows // TILE)
    heads = range(N_HEADS)
    chains = [(t, h) for t in tiles for h in heads]
    rsl = [slice(t * TILE, (t + 1) * TILE) for t in tiles]

    x, lak, mrb, mrk = {}, {}, {}, {}
    for t in tiles:
        at, rt = at_s[rsl[t], :], rt_s[rsl[t], :]
        rhs_nt = jnp.concatenate([bt_s[rsl[t], :], kt_s[rsl[t], :]], axis=0).astype(BF16)
        for h in heads:
            sc = _dot_nt(jnp.concatenate([at * hm[h], rt * hm[h]], axis=0), rhs_nt)
            x[t, h] = jnp.where(strict, -sc[0:TILE, 0:TILE], 0.0)
            lak[t, h] = jnp.where(strict, sc[0:TILE, TILE:2 * TILE], 0.0).astype(BF16)
            mrb[t, h] = jnp.where(incl, sc[TILE:2 * TILE, 0:TILE], 0.0).astype(BF16)
            mrk[t, h] = jnp.where(incl, sc[TILE:2 * TILE, TILE:2 * TILE], 0.0).astype(BF16)
    p = _unit_lower_inverses(x, eye, chains)
    for t in tiles:
        at = at_s[rsl[t], :]
        vt = v_s[rsl[t], :].astype(BF16)
        wa = jnp.zeros((TILE, GROUP_W), F32)
        u0 = jnp.zeros((TILE, GROUP_W), F32)
        y0 = jnp.zeros((TILE, GROUP_W), F32)
        for h in heads:
            sol = _dot(p[t, h], jnp.concatenate([at, _dot(lak[t, h], vt)], axis=1))
            wa = wa + hm[h] * sol[:, 0:GROUP_W]
            u0 = u0 + hm[h] * sol[:, GROUP_W:2 * GROUP_W]
            y0 = y0 + hm[h] * _dot(mrk[t, h], vt)
        wa_s[rsl[t], :] = wa
        u_s[rsl[t], :] = u0
        y_s[rsl[t], :] = y0

    s = s_ref[...]
    for c in range(rows // CHUNK):
        cs = slice(c * CHUNK, (c + 1) * CHUNK)
        u_c = _dot_nt(wa_s[cs, :], s) + u_s[cs, :]
        y_s[cs, :] = y_s[cs, :] + _dot_nt(rt_s[cs, :], s)
        u_s[cs, :] = u_c
        upd = _dot_tn(jnp.concatenate([u_c, v_s[cs, :]], axis=0),
                      jnp.concatenate([btl_s[cs, :], ktl_s[cs, :]], axis=0))
        s = s * gam_s[c * CHUNK:c * CHUNK + 1, :] + bd * upd
    s_ref[...] = s

    for t in tiles:
        u = u_s[rsl[t], :].astype(BF16)
        y = y_s[rsl[t], :]
        for h in heads:
            y = y + hm[h] * _dot(mrb[t, h], u)
        mean = _seg_sum(y, seg) * (1.0 / HEAD_DIM)
        yc = y - mean
        var = _seg_sum(yc * yc, seg) * (1.0 / HEAD_DIM)
        yn = yc * lax.rsqrt(var + RW_GN_EPS) * gng + gnb
        o_ref[rsl[t], :] = ((yn + bon_s[rsl[t], :]) * gate_s[rsl[t], :]).astype(o_ref.dtype)


def _rwkv(prw, mu, w0, w_up, a0, a_up, g_up, k_k, k_a, r_k, gn_g, gn_b, bn, seq):
    t = prw.shape[0]
    rows = ROW_BLOCK
    nb = seq // rows
    _, _, _, seg = _expand_mats()
    tri, _, full = _chunk_mats(rows)
    row = lambda x: x.reshape(1, -1).astype(F32)
    consts = (row(mu), row(w0), w_up.astype(BF16), row(a0), a_up.astype(BF16), g_up.astype(BF16),
              row(k_k), row(k_a), row(r_k), row(gn_g), row(gn_b), seg, tri, full)
    rowmap = lambda b, j: (b * nb + j, 0)
    big = lambda: pltpu.VMEM((rows, GROUP_W), F32)
    return pl.pallas_call(
        _rwkv_kernel,
        out_shape=jax.ShapeDtypeStruct((t, GROUP_W), BF16),
        grid=(bn, nb),
        in_specs=[pl.BlockSpec((rows, 4 * GROUP_W), rowmap)]
                 + [pl.BlockSpec(c.shape, lambda b, j: (0, 0)) for c in consts],
        out_specs=pl.BlockSpec((rows, GROUP_W), rowmap),
        scratch_shapes=[pltpu.VMEM((rows + 8, 4 * GROUP_W), F32), pltpu.VMEM((GROUP_W, GROUP_W), F32)]
                       + [big() for _ in range(13)],
        compiler_params=_cparams(("arbitrary", "arbitrary")),
        name="rwkv7_mixer",
    )(prw, *consts)


def _out_router_kernel(x_ref, oa_ref, obc_ref, od_ref, wo_ref, gt_ref, g_ref, sh_ref, sc_ref,
                       wrh_ref, wrl_ref, br_ref, tri_ref,
                       xo_ref, hf_ref, route_ref, cnt_ref, carry):
    i = pl.program_id(0)

    @pl.when(i == 0)
    def _():
        carry[...] = jnp.zeros(carry.shape, F32)

    g_w = GROUP_W
    mixed = jnp.dot(oa_ref[...], wo_ref[0:g_w, :], preferred_element_type=F32)
    mixed += jnp.dot(obc_ref[...], wo_ref[g_w:3 * g_w, :], preferred_element_type=F32)
    mixed += jnp.dot(od_ref[...], wo_ref[3 * g_w:4 * g_w, :], preferred_element_type=F32)
    x = x_ref[...] + gt_ref[0] * mixed
    xo_ref[...] = x
    y = x * lax.rsqrt(jnp.mean(x * x, -1, keepdims=True) + NORM_EPS) * g_ref[...]
    hf = y * (1.0 + sc_ref[0]) + sh_ref[0]
    hf_ref[...] = hf
    hh, hl = _split2(hf)
    lg = (jnp.dot(hh, wrh_ref[...], preferred_element_type=F32) + jnp.dot(hl, wrh_ref[...], preferred_element_type=F32)
          + jnp.dot(hh, wrl_ref[...], preferred_element_type=F32) + br_ref[...])
    tm = lg.shape[0]
    lane = lax.broadcasted_iota(jnp.int32, (tm, LANES), 1)
    lanef = lane.astype(F32)
    big = jnp.float32(1e9)
    ninf = jnp.float32(-jnp.inf)
    is_g = (lane >= N_EXPERTS) & (lane < N_EXPERTS + N_GROUPS)
    gl = jnp.where(is_g, lg, ninf)
    gmax = jnp.max(gl, -1, keepdims=True)
    gsel = jnp.min(jnp.where(gl == gmax, lanef - N_EXPERTS, big), -1, keepdims=True)
    p_group = 1.0 / jnp.sum(jnp.where(is_g, jnp.exp(gl - gmax), 0.0), -1, keepdims=True)
    in_grp = (lane < N_EXPERTS) & ((lane >> 3).astype(F32) == gsel)
    el = jnp.where(in_grp, lg, ninf)
    v1 = jnp.max(el, -1, keepdims=True)
    i1 = jnp.min(jnp.where(el == v1, lanef, big), -1, keepdims=True)
    el2 = jnp.where(lanef == i1, ninf, el)
    v2 = jnp.max(el2, -1, keepdims=True)
    i2 = jnp.min(jnp.where(el2 == v2, lanef, big), -1, keepdims=True)
    e21 = jnp.exp(v2 - v1)
    g1 = p_group / (1.0 + e21)
    g2 = p_group * e21 / (1.0 + e21)
    oh1 = lanef == i1
    oh2 = lanef == i2
    oh = jnp.where(oh1 | oh2, 1.0, 0.0)
    total = jnp.dot(tri_ref[...], oh.astype(BF16), preferred_element_type=F32) + carry[...]
    rank1 = jnp.sum(jnp.where(oh1, total, 0.0), -1, keepdims=True)
    rank2 = jnp.sum(jnp.where(oh2, total, 0.0), -1, keepdims=True)
    carry[...] = carry[...] + jnp.sum(oh, axis=0, keepdims=True)
    cnt_ref[...] = carry[...]
    route = jnp.where(lane == 0, i1, jnp.where(lane == 1, i2, jnp.where(lane == 2, rank1, jnp.where(
        lane == 3, rank2, jnp.where(lane == 4, g1, jnp.where(lane == 5, g2, 0.0))))))
    route_ref[...] = route


def _out_router(x, oa, obc, od, w_out, gt, g, shift, scale, wr_hi, wr_lo, b_r, seq):
    t, d = x.shape
    tm = ROW_BLOCK
    per_b = seq // tm
    tri = (jnp.arange(tm)[:, None] > jnp.arange(tm)[None, :]).astype(BF16)
    bspec = pl.BlockSpec((1, 1, d), lambda i: (i // per_b, 0, 0))
    full = lambda a: pl.BlockSpec(a.shape, lambda i: (0,) * a.ndim)
    return pl.pallas_call(
        _out_router_kernel,
        out_shape=(jax.ShapeDtypeStruct((t, d), F32), jax.ShapeDtypeStruct((t, d), F32),
                   jax.ShapeDtypeStruct((t, LANES), F32), jax.ShapeDtypeStruct((1, LANES), F32)),
        grid=(t // tm,),
        in_specs=[pl.BlockSpec((tm, d), lambda i: (i, 0)),
                  pl.BlockSpec((tm, GROUP_W), lambda i: (i, 0)),
                  pl.BlockSpec((tm, 2 * GROUP_W), lambda i: (i, 0)),
                  pl.BlockSpec((tm, GROUP_W), lambda i: (i, 0)),
                  full(w_out), bspec, full(g), bspec, bspec, full(wr_hi), full(wr_lo), full(b_r), full(tri)],
        out_specs=(pl.BlockSpec((tm, d), lambda i: (i, 0)), pl.BlockSpec((tm, d), lambda i: (i, 0)),
                   pl.BlockSpec((tm, LANES), lambda i: (i, 0)), pl.BlockSpec((1, LANES), lambda i: (0, 0))),
        scratch_shapes=[pltpu.VMEM((1, LANES), F32)],
        compiler_params=_cparams(("arbitrary",)),
        name="out_proj_router",
    )(x, oa, obc, od, w_out, gt, g, shift, scale, wr_hi, wr_lo, b_r, tri)


def _row_copy(src, src_row, dst, dst_row, sem):
    return pltpu.make_async_copy(src.at[pl.ds(src_row, 1), :], dst.at[pl.ds(dst_row, 1), :], sem)


DMA_UNROLL = 8


def _dest_row(ps_ref, rt_ref, r, slot):
    return ps_ref[rt_ref[0, 0, 4 * r + slot]] + rt_ref[0, 0, 4 * r + 2 + slot]


def _dispatch_kernel(ps_ref, rt_ref, hf_ref, xs_in_ref, xs_ref, sem):
    del xs_in_ref
    n_tok = hf_ref.shape[0]

    def issue(g, carry):
        for uu in range(DMA_UNROLL):
            r = g * DMA_UNROLL + uu
            _row_copy(hf_ref, r, xs_ref, _dest_row(ps_ref, rt_ref, r, 0), sem).start()
            _row_copy(hf_ref, r, xs_ref, _dest_row(ps_ref, rt_ref, r, 1), sem).start()
        return carry

    lax.fori_loop(0, n_tok // DMA_UNROLL, issue, 0)
    for _ in range(2):
        pltpu.make_async_copy(hf_ref, xs_ref.at[pl.ds(0, n_tok), :], sem).wait()


def _dispatch(pad_start, rt3, hf, p_rows):
    t, d = hf.shape
    td = MOE_TOK
    xs0 = jnp.zeros((p_rows, d), F32)
    grid_spec = pltpu.PrefetchScalarGridSpec(
        num_scalar_prefetch=1,
        grid=(t // td,),
        in_specs=[pl.BlockSpec((1, 1, 4 * td), lambda i, ps: (i, 0, 0), memory_space=pltpu.SMEM),
                  pl.BlockSpec((td, d), lambda i, ps: (i, 0)),
                  pl.BlockSpec(memory_space=pl.ANY)],
        out_specs=pl.BlockSpec(memory_space=pl.ANY),
        scratch_shapes=[pltpu.SemaphoreType.DMA(())],
    )
    return pl.pallas_call(
        _dispatch_kernel,
        out_shape=jax.ShapeDtypeStruct((p_rows, d), F32),
        grid_spec=grid_spec,
        input_output_aliases={3: 0},
        compiler_params=_cparams(("arbitrary",)),
        name="moe_dispatch",
    )(pad_start, rt3, hf, xs0)


def _expert_kernel(be_ref, nu_ref, xs_ref, wg_ref, wu_ref, wd_ref, ys_ref, wg_b, wu_b, wd_b):
    i = pl.program_id(0)
    used = i < nu_ref[0]
    new_expert = (i == 0) | (be_ref[i] != be_ref[jnp.maximum(i - 1, 0)])

    @pl.when(used & new_expert)
    def _():
        wg_b[...] = wg_ref[...].astype(BF16)
        wu_b[...] = wu_ref[...].astype(BF16)
        wd_b[...] = wd_ref[...].astype(BF16)

    @pl.when(used)
    def _():
        xb = xs_ref[...].astype(BF16)
        hid = _silu(jnp.dot(xb, wg_b[...], preferred_element_type=F32)) * jnp.dot(xb, wu_b[...], preferred_element_type=F32)
        ys_ref[...] = jnp.dot(hid.astype(BF16), wd_b[...], preferred_element_type=F32)

    @pl.when(jnp.logical_not(used))
    def _():
        ys_ref[...] = jnp.zeros(ys_ref.shape, F32)


def _experts(blk_e, n_used, xs, w_gate, w_up, w_down, layer):
    p_rows, d = xs.shape
    bm = MOE_ROWS
    grid_spec = pltpu.PrefetchScalarGridSpec(
        num_scalar_prefetch=2,
        grid=(p_rows // bm,),
        in_specs=[pl.BlockSpec((bm, d), lambda i, be, nu: (i, 0)),
                  pl.BlockSpec((None, None, d, D_EXPERT), lambda i, be, nu: (layer, be[i], 0, 0)),
                  pl.BlockSpec((None, None, d, D_EXPERT), lambda i, be, nu: (layer, be[i], 0, 0)),
                  pl.BlockSpec((None, None, D_EXPERT, d), lambda i, be, nu: (layer, be[i], 0, 0))],
        out_specs=pl.BlockSpec((bm, d), lambda i, be, nu: (i, 0)),
        scratch_shapes=[pltpu.VMEM((d, D_EXPERT), BF16), pltpu.VMEM((d, D_EXPERT), BF16),
                        pltpu.VMEM((D_EXPERT, d), BF16)],
    )
    return pl.pallas_call(
        _expert_kernel,
        out_shape=jax.ShapeDtypeStruct((p_rows, d), F32),
        grid_spec=grid_spec,
        compiler_params=_cparams(("arbitrary",)),
        name="moe_experts",
    )(blk_e, n_used, xs, w_gate, w_up, w_down)


def _combine_kernel(ps_ref, rt_ref, route_ref, x_ref, gt_ref, fg_ref, ys_ref, o_ref, y1, y2, sem, *, final):
    n_tok = x_ref.shape[0]

    def issue(g, carry):
        for uu in range(DMA_UNROLL):
            r = g * DMA_UNROLL + uu
            _row_copy(ys_ref, _dest_row(ps_ref, rt_ref, r, 0), y1, r, sem).start()
            _row_copy(ys_ref, _dest_row(ps_ref, rt_ref, r, 1), y2, r, sem).start()
        return carry

    lax.fori_loop(0, n_tok // DMA_UNROLL, issue, 0)
    pltpu.make_async_copy(ys_ref.at[pl.ds(0, n_tok), :], y1, sem).wait()
    pltpu.make_async_copy(ys_ref.at[pl.ds(0, n_tok), :], y2, sem).wait()
    route = route_ref[...]
    y = route[:, 4:5] * y1[...] + route[:, 5:6] * y2[...]
    x = x_ref[...] + gt_ref[0] * y
    if final:
        x = x * lax.rsqrt(jnp.mean(x * x, -1, keepdims=True) + NORM_EPS) * fg_ref[...]
    o_ref[...] = x


def _combine(pad_start, rt3, route, x, gt, final_g, ys, seq, final):
    t, d = x.shape
    tc = MOE_TOK
    per_b = seq // tc
    grid_spec = pltpu.PrefetchScalarGridSpec(
        num_scalar_prefetch=1,
        grid=(t // tc,),
        in_specs=[pl.BlockSpec((1, 1, 4 * tc), lambda i, ps: (i, 0, 0), memory_space=pltpu.SMEM),
                  pl.BlockSpec((tc, LANES), lambda i, ps: (i, 0)),
                  pl.BlockSpec((tc, d), lambda i, ps: (i, 0)),
                  pl.BlockSpec((1, 1, d), lambda i, ps: (i // per_b, 0, 0)),
                  pl.BlockSpec((1, d), lambda i, ps: (0, 0)),
                  pl.BlockSpec(memory_space=pl.ANY)],
        out_specs=pl.BlockSpec((tc, d), lambda i, ps: (i, 0)),
        scratch_shapes=[pltpu.VMEM((tc, d), F32), pltpu.VMEM((tc, d), F32), pltpu.SemaphoreType.DMA(())],
    )
    return pl.pallas_call(
        functools.partial(_combine_kernel, final=final),
        out_shape=jax.ShapeDtypeStruct((t, d), F32),
        grid_spec=grid_spec,
        compiler_params=_cparams(("arbitrary",)),
        name="moe_combine",
    )(pad_start, rt3, route, x, gt, final_g, ys)


def _route_plan(route, counts, t):
    bm = MOE_ROWS
    cnt = counts[0, :N_EXPERTS].astype(jnp.int32)
    padded = (cnt + bm - 1) // bm * bm
    pad_end = jnp.cumsum(padded)
    pad_start = pad_end - padded
    p_rows = 2 * t + N_EXPERTS * bm
    n_blk = p_rows // bm
    blk_start = jnp.arange(n_blk, dtype=jnp.int32) * bm
    blk_e = jnp.minimum(jnp.sum((pad_end[None, :] <= blk_start[:, None]).astype(jnp.int32), axis=1), N_EXPERTS - 1)
    n_used = (pad_end[-1:] // bm).astype(jnp.int32)
    rt3 = route[:, 0:4].astype(jnp.int32).reshape(t // MOE_TOK, 1, 4 * MOE_TOK)
    return pad_start, rt3, blk_e, n_used, p_rows


def kernel(x, c, ada_w, ada_b, mix_norm_g, ffn_norm_g, w_in, w_out, gdn_conv_w, gdn_a_log, gdn_dt_bias, gdn_norm_g,
           sgu_ln_g, sgu_ln_b, sgu_w, sgu_b, sc_conv_w, rw_mu, rw_w0, rw_w_up, rw_a0, rw_a_up, rw_g_up, rw_k_k,
           rw_k_a, rw_r_k, rw_gn_g, rw_gn_b, moe_w_group, moe_b_group, moe_w_router, moe_b_router, moe_w_gate,
           moe_w_up, moe_w_down, final_norm_g):
    bn, seq, d = x.shape
    depth = ada_w.shape[0]
    t = bn * seq
    assert d == 4 * GROUP_W and seq % ROW_BLOCK == 0 and ROW_BLOCK % MOE_TOK == 0
    g_w = GROUP_W
    mod = _ada(c, ada_w, ada_b)
    xf = x.reshape(t, d)
    o_z, o_a, o_su = 3 * g_w, 4 * g_w, 4 * g_w + 2 * N_HEADS
    o_rp = o_su + 5 * g_w
    for l in range(depth):
        m = mod[l].reshape(bn, 6, 1, d)
        sh_m, sc_m, gt_m, sh_f, sc_f, gt_f = (m[:, i] for i in range(6))
        wl = w_in[l]
        w_ab = wl[:, o_a:o_su]
        w_r = jnp.concatenate([wl[:, 0:o_a], wl[:, o_su:o_rp], wl[:, o_rp:],
                               jnp.pad(w_ab, ((0, 0), (0, LANES - 2 * N_HEADS)))], axis=1).astype(BF16)
        pg, psc, prw, pab, abt = _in_proj(xf, mix_norm_g[l][None, :], sh_m, sc_m, w_r, w_ab.T.astype(BF16), seq)
        oa = _gdn(pg, pab, abt, gdn_conv_w[l], gdn_a_log[l], gdn_dt_bias[l], gdn_norm_g[l], bn, seq)
        obc = _sgu_conv(psc, sgu_ln_g[l], sgu_ln_b[l], sgu_w[l], sgu_b[l], sc_conv_w[l], bn, seq)
        od = _rwkv(prw, rw_mu[l], rw_w0[l], rw_w_up[l], rw_a0[l], rw_a_up[l], rw_g_up[l], rw_k_k[l], rw_k_a[l],
                   rw_r_k[l], rw_gn_g[l], rw_gn_b[l], bn, seq)
        w_rt = jnp.concatenate([moe_w_router[l], moe_w_group[l],
                                jnp.zeros((d, LANES - N_EXPERTS - N_GROUPS), F32)], axis=1)
        wr_hi = w_rt.astype(BF16)
        wr_lo = (w_rt - wr_hi.astype(F32)).astype(BF16)
        b_r = jnp.concatenate([moe_b_router[l], moe_b_group[l], jnp.zeros((LANES - N_EXPERTS - N_GROUPS,), F32)])[None, :]
        xf, hf, route, counts = _out_router(xf, oa, obc, od, w_out[l].astype(BF16), gt_m, ffn_norm_g[l][None, :],
                                            sh_f, sc_f, wr_hi, wr_lo, b_r, seq)
        pad_start, rt3, blk_e, n_used, p_rows = _route_plan(route, counts, t)
        xs = _dispatch(pad_start, rt3, hf, p_rows)
        ys = _experts(blk_e, n_used, xs, moe_w_gate, moe_w_up, moe_w_down, l)
        xf = _combine(pad_start, rt3, route, xf, gt_f, final_norm_g[None, :], ys, seq, final=(l == depth - 1))
    return xf.reshape(bn, seq, d)
```

```python
import functools
import math

import jax
import jax.numpy as jnp
from jax import lax
from jax.experimental import pallas as pl
from jax.experimental.pallas import tpu as pltpu

F32 = jnp.float32
BF16 = jnp.bfloat16

HEAD_DIM = 64
N_HEADS = 4
GROUP_W = HEAD_DIM * N_HEADS
CHUNK = 64
TILE = 2 * CHUNK
NORM_EPS = 1e-6
SGU_CHUNK = 128
SGU_LN_EPS = 1e-5
RW_GN_EPS = 64e-5
RW_LORA_W, RW_LORA_A, RW_LORA_G = 64, 64, 128
N_GROUPS = 8
EXPERTS_PER_GROUP = 8
N_EXPERTS = N_GROUPS * EXPERTS_PER_GROUP
D_EXPERT = 256
PAIRS_PER_GROUP = EXPERTS_PER_GROUP * (EXPERTS_PER_GROUP - 1) // 2
N_BUCKETS = N_GROUPS * PAIRS_PER_GROUP
LANES = 128
VMEM_LIMIT = 56 * 1024 * 1024

ROW_BLOCK = 512
MOE_ROWS = 128
MOE_TOK = 256


def _dot(a, b):
    return jnp.dot(a.astype(BF16), b.astype(BF16), preferred_element_type=F32)


def _dot_nt(a, b):
    return lax.dot_general(a.astype(BF16), b.astype(BF16), (((1,), (1,)), ((), ())),
                           preferred_element_type=F32)


def _dot_tn(a, b):
    return lax.dot_general(a.astype(BF16), b.astype(BF16), (((0,), (0,)), ((), ())),
                           preferred_element_type=F32)


def _split2(x):
    hi = x.astype(BF16)
    lo = (x - hi.astype(F32)).astype(BF16)
    return hi, lo


def _dot_x_exact(x, m):
    hi, lo = _split2(x)
    return (jnp.dot(hi, m, preferred_element_type=F32) + jnp.dot(lo, m, preferred_element_type=F32))


def _dot_exact_x(m, x):
    hi, lo = _split2(x)
    return (jnp.dot(m, hi, preferred_element_type=F32) + jnp.dot(m, lo, preferred_element_type=F32))


def _sigmoid(x):
    return 1.0 / (1.0 + jnp.exp(-x))


def _silu(x):
    return x * _sigmoid(x)


def _softplus(x):
    return jnp.maximum(x, 0.0) + jnp.log(1.0 + jnp.exp(-jnp.abs(x)))


def _head_masks():
    lane = lax.broadcasted_iota(jnp.int32, (1, GROUP_W), 1)
    return [((lane >> 6) == h).astype(F32) for h in range(N_HEADS)]


def _tile_masks():
    ri = lax.broadcasted_iota(jnp.int32, (TILE, TILE), 0)
    ci = lax.broadcasted_iota(jnp.int32, (TILE, TILE), 1)
    same = (ri >> 6) == (ci >> 6)
    return same & (ri > ci), same & (ri >= ci), (ri == ci).astype(F32)


def _block_diag_mask():
    ri = lax.broadcasted_iota(jnp.int32, (GROUP_W, GROUP_W), 0)
    ci = lax.broadcasted_iota(jnp.int32, (GROUP_W, GROUP_W), 1)
    return ((ri >> 6) == (ci >> 6)).astype(F32)


def _merge_masks():
    ri = lax.broadcasted_iota(jnp.int32, (TILE, TILE), 0)
    ci = lax.broadcasted_iota(jnp.int32, (TILE, TILE), 1)
    return [((ri >> (l + 1)) == (ci >> (l + 1))) & (((ri >> l) & 1) == 1) & (((ci >> l) & 1) == 0) for l in range(6)]


def _unit_lower_inverses(a, eye, chains):
    masks = _merge_masks()
    d = {c: eye - jnp.where(masks[0], a[c], 0.0) for c in chains}
    for l in range(1, 6):
        f = {c: _dot(jnp.where(masks[l], a[c], 0.0), d[c]) for c in chains}
        d = {c: d[c] - _dot(d[c], f[c]) for c in chains}
    return d


def _seg_sum(x, seg):
    return jnp.dot(x.astype(BF16), seg, preferred_element_type=F32)


def _cparams(sem):
    return pltpu.CompilerParams(dimension_semantics=sem, vmem_limit_bytes=VMEM_LIMIT)


def _ada_kernel(c_ref, w_ref, b_ref, o_ref):
    c = c_ref[...]
    ca = _silu(c)
    chi, clo = _split2(ca)
    w = w_ref[...]
    whi, wlo = _split2(w)
    acc = jnp.dot(chi, whi, preferred_element_type=F32)
    acc += jnp.dot(clo, whi, preferred_element_type=F32)
    acc += jnp.dot(chi, wlo, preferred_element_type=F32)
    o_ref[...] = acc + b_ref[...]


def _ada(c, ada_w, ada_b):
    depth, d, d6 = ada_w.shape
    bn = c.shape[0]
    nj = d6 // d
    return pl.pallas_call(
        _ada_kernel,
        out_shape=jax.ShapeDtypeStruct((depth, bn, d6), F32),
        grid=(depth, nj),
        in_specs=[pl.BlockSpec((bn, d), lambda l, j: (0, 0)),
                  pl.BlockSpec((None, d, d), lambda l, j: (l, 0, j)),
                  pl.BlockSpec((None, 1, d), lambda l, j: (l, 0, j))],
        out_specs=pl.BlockSpec((None, bn, d), lambda l, j: (l, 0, j)),
        compiler_params=_cparams(("arbitrary", "arbitrary")),
        name="ada_mod",
    )(c, ada_w, ada_b.reshape(depth, 1, d6))


def _in_proj_kernel(x_ref, g_ref, sh_ref, sc_ref, w_ref, wab_ref, pg_ref, psc_ref, prw_ref, pab_ref, abt_ref):
    x = x_ref[...]
    y = x * lax.rsqrt(jnp.mean(x * x, -1, keepdims=True) + NORM_EPS) * g_ref[...]
    h = (y * (1.0 + sc_ref[0]) + sh_ref[0]).astype(BF16)
    o = 0
    for ref in (pg_ref, psc_ref, prw_ref, pab_ref):
        w = ref.shape[1]
        ref[...] = jnp.dot(h, w_ref[:, o:o + w], preferred_element_type=F32)
        o += w
    abt_ref[...] = lax.dot_general(wab_ref[...], h, (((1,), (1,)), ((), ())), preferred_element_type=F32)


def _in_proj(x, g, shift, scale, w_r, w_abt, seq):
    t, d = x.shape
    tm = ROW_BLOCK
    per_b = seq // tm
    widths = (4 * GROUP_W, 5 * GROUP_W, 4 * GROUP_W, LANES)
    bspec = pl.BlockSpec((1, 1, d), lambda i: (i // per_b, 0, 0))
    return pl.pallas_call(
        _in_proj_kernel,
        out_shape=tuple(jax.ShapeDtypeStruct((t, w), F32) for w in widths) + (jax.ShapeDtypeStruct((8, t), F32),),
        grid=(t // tm,),
        in_specs=[pl.BlockSpec((tm, d), lambda i: (i, 0)),
                  pl.BlockSpec((1, d), lambda i: (0, 0)),
                  bspec, bspec,
                  pl.BlockSpec(w_r.shape, lambda i: (0, 0)),
                  pl.BlockSpec(w_abt.shape, lambda i: (0, 0))],
        out_specs=tuple(pl.BlockSpec((tm, w), lambda i: (i, 0)) for w in widths) + (pl.BlockSpec((8, tm), lambda i: (0, i)),),
        compiler_params=_cparams(("arbitrary",)),
        name="in_proj",
    )(x, g, shift, scale, w_r, w_abt)


def _gdn_kernel(p_ref, ab_ref, abt_ref, cw_ref, alog_ref, dtb_ref, alogt_ref, dtbt_ref, ng_ref,
                eg_ref, eb_ref, egx_ref, seg_ref, tri_ref, trit_ref, full_ref,
                o_ref,
                xbuf, s_ref, q_s, k_s, kb_s, rhs_s, qd_s, kt_s, cd_s, gcx_s, gct_s, u_s, w_s, snap_s):
    rows = p_ref.shape[0]
    j = pl.program_id(1)

    @pl.when(j == 0)
    def _():
        xbuf[0:8, :] = jnp.zeros((8, xbuf.shape[1]), F32)
        s_ref[...] = jnp.zeros(s_ref.shape, F32)

    xbuf[8:8 + rows, :] = p_ref[:, 0:3 * GROUP_W]
    acc = cw_ref[3:4, :] * xbuf[8:8 + rows, :]
    for tap in range(3):
        acc = acc + cw_ref[tap:tap + 1, :] * xbuf[5 + tap:5 + tap + rows, :]
    xbuf[0:8, :] = xbuf[rows:rows + 8, :]
    qkv = _silu(acc)
    seg = seg_ref[...]
    q = qkv[:, 0:GROUP_W]
    k = qkv[:, GROUP_W:2 * GROUP_W]
    v = qkv[:, 2 * GROUP_W:3 * GROUP_W]
    q = q * lax.rsqrt(_seg_sum(q * q, seg) + 1e-6) * (HEAD_DIM ** -0.5)
    k = k * lax.rsqrt(_seg_sum(k * k, seg) + 1e-6)

    ab = ab_ref[...]
    g = -jnp.exp(alog_ref[...]) * _softplus(ab + dtb_ref[...])
    beta = _dot_x_exact(_sigmoid(ab), eb_ref[...])
    gc = _dot_exact_x(tri_ref[...], _dot_x_exact(g, eg_ref[...]))
    gl = _dot_exact_x(full_ref[...], _dot_x_exact(g, eg_ref[...]))
    egc = jnp.exp(gc)
    kb = k * beta
    q_s[...] = q
    k_s[...] = k
    kb_s[...] = kb
    rhs_s[:, 0:GROUP_W] = v * beta
    rhs_s[:, GROUP_W:2 * GROUP_W] = kb * egc
    qd_s[...] = q * egc
    kt_s[...] = k * jnp.exp(gl - gc)
    cd_s[...] = jnp.exp(gl)
    gcx_s[...] = _dot_exact_x(tri_ref[...], _dot_x_exact(g, egx_ref[...]))
    abt = abt_ref[...]
    gt = -jnp.exp(alogt_ref[...]) * _softplus(abt + dtbt_ref[...])
    gct = _dot_x_exact(gt, trit_ref[...])
    for t in range(rows // TILE):
        gct_s[t] = gct[:, t * TILE:(t + 1) * TILE]

    hm = _head_masks()
    strict, incl, eye = _tile_masks()
    bd = _block_diag_mask()
    ng = ng_ref[...]
    tiles = range(rows // TILE)
    heads = range(N_HEADS)
    chains = [(t, h) for t in tiles for h in heads]
    rsl = [slice(t * TILE, (t + 1) * TILE) for t in tiles]

    x, attn = {}, {}
    for t in tiles:
        kt_ = k_s[rsl[t], :].astype(BF16)
        kbt, qt = kb_s[rsl[t], :], q_s[rsl[t], :]
        gct_t = gct_s[t]
        for h in heads:
            dec = jnp.exp(jnp.where(incl, gcx_s[rsl[t], h * LANES:(h + 1) * LANES] - gct_t[h:h + 1, :], -jnp.inf))
            x[t, h] = jnp.where(strict, _dot_nt(kbt * hm[h], kt_) * dec, 0.0)
            attn[t, h] = (_dot_nt(qt * hm[h], kt_) * dec).astype(BF16)
    p = _unit_lower_inverses(x, eye, chains)
    for t in tiles:
        rhs = rhs_s[rsl[t], :].astype(BF16)
        u = jnp.zeros((TILE, GROUP_W), F32)
        w = jnp.zeros((TILE, GROUP_W), F32)
        for h in heads:
            sol = _dot(p[t, h], rhs)
            u = u + hm[h] * sol[:, 0:GROUP_W]
            w = w + hm[h] * sol[:, GROUP_W:2 * GROUP_W]
        u_s[rsl[t], :] = u
        w_s[rsl[t], :] = w

    chunks = range(rows // CHUNK)
    csl = [slice(c * CHUNK, (c + 1) * CHUNK) for c in chunks]
    pq = [_dot_tn(kt_s[csl[c], :], jnp.concatenate([w_s[csl[c], :], u_s[csl[c], :]], axis=1)) for c in chunks]
    pmat = [(bd * pq[c][:, 0:GROUP_W]).astype(BF16) for c in chunks]
    s = s_ref[...]
    for c in chunks:
        snap_s[c] = s.astype(BF16)
        s = s * cd_s[c * CHUNK:c * CHUNK + 1, :] - _dot(pmat[c], s) + bd * pq[c][:, GROUP_W:2 * GROUP_W]
    s_ref[...] = s
    for c in chunks:
        ws = jnp.dot(jnp.concatenate([w_s[csl[c], :], qd_s[csl[c], :]], axis=0).astype(BF16), snap_s[c],
                     preferred_element_type=F32)
        u_s[csl[c], :] = u_s[csl[c], :] - ws[0:CHUNK]
        w_s[csl[c], :] = ws[CHUNK:2 * CHUNK]

    for t in tiles:
        vn = u_s[rsl[t], :].astype(BF16)
        o = w_s[rsl[t], :]
        for h in heads:
            o = o + hm[h] * _dot(attn[t, h], vn)
        o = o * lax.rsqrt(_seg_sum(o * o, seg) * (1.0 / HEAD_DIM) + NORM_EPS) * ng
        o_ref[rsl[t], :] = (o * _silu(p_ref[rsl[t], 3 * GROUP_W:4 * GROUP_W])).astype(o_ref.dtype)


def _chunk_mats(rows):
    ri = jnp.arange(rows)[:, None]
    ci = jnp.arange(rows)[None, :]
    same = (ri // CHUNK) == (ci // CHUNK)
    tri = (same & (ci <= ri)).astype(BF16)
    return tri, tri.T, same.astype(BF16)


def _expand_mats():
    lane = jnp.arange(LANES)[:, None]
    col = jnp.arange(GROUP_W)[None, :]
    eg = (lane == col // HEAD_DIM).astype(BF16)
    eb = (lane == N_HEADS + col // HEAD_DIM).astype(BF16)
    colx = jnp.arange(N_HEADS * LANES)[None, :]
    egx = (lane == colx // LANES).astype(BF16)
    seg = ((jnp.arange(GROUP_W)[:, None] // HEAD_DIM) == (col // HEAD_DIM)).astype(BF16)
    return eg, eb, egx, seg


def _pad_lanes(v, n=LANES):
    return jnp.zeros((1, n), F32).at[0, :v.shape[0]].set(v.astype(F32))


def _gdn(pg, pab, abt, conv_w, a_log, dt_bias, norm_g, bn, seq):
    t = pg.shape[0]
    rows = ROW_BLOCK
    nb = seq // rows
    eg, eb, egx, seg = _expand_mats()
    tri, trit, full = _chunk_mats(rows)
    alog_t = jnp.zeros((8, rows), F32).at[:N_HEADS].set(jnp.broadcast_to(a_log[:, None], (N_HEADS, rows)))
    dtb_t = jnp.zeros((8, rows), F32).at[:N_HEADS].set(jnp.broadcast_to(dt_bias[:, None], (N_HEADS, rows)))
    consts = (conv_w.astype(F32), _pad_lanes(a_log), _pad_lanes(dt_bias), alog_t, dtb_t,
              jnp.tile(norm_g.astype(F32), N_HEADS)[None, :], eg, eb, egx, seg, tri, trit, full)
    rowmap = lambda b, j: (b * nb + j, 0)
    return pl.pallas_call(
        _gdn_kernel,
        out_shape=jax.ShapeDtypeStruct((t, GROUP_W), BF16),
        grid=(bn, nb),
        in_specs=[pl.BlockSpec((rows, 4 * GROUP_W), rowmap),
                  pl.BlockSpec((rows, LANES), rowmap),
                  pl.BlockSpec((8, rows), lambda b, j: (0, b * nb + j))]
                 + [pl.BlockSpec(c.shape, lambda b, j: (0, 0)) for c in consts],
        out_specs=pl.BlockSpec((rows, GROUP_W), rowmap),
        scratch_shapes=[pltpu.VMEM((rows + 8, 3 * GROUP_W), F32),
                        pltpu.VMEM((GROUP_W, GROUP_W), F32),
                        pltpu.VMEM((rows, GROUP_W), F32), pltpu.VMEM((rows, GROUP_W), F32),
                        pltpu.VMEM((rows, GROUP_W), F32), pltpu.VMEM((rows, 2 * GROUP_W), F32),
                        pltpu.VMEM((rows, GROUP_W), F32), pltpu.VMEM((rows, GROUP_W), F32),
                        pltpu.VMEM((rows, GROUP_W), F32), pltpu.VMEM((rows, N_HEADS * LANES), F32),
                        pltpu.VMEM((rows // TILE, 8, TILE), F32),
                        pltpu.VMEM((rows, GROUP_W), F32), pltpu.VMEM((rows, GROUP_W), F32),
                        pltpu.VMEM((rows // CHUNK, GROUP_W, GROUP_W), BF16)],
        compiler_params=_cparams(("arbitrary", "arbitrary")),
        name="gdn_mixer",
    )(pg, pab, abt, *consts)


def _sgu_conv_kernel(p_ref, lng_ref, lnb_ref, ws_ref, bs_ref, cw_ref, o_ref, xbuf):
    rows = p_ref.shape[0]
    j = pl.program_id(1)

    @pl.when(j == 0)
    def _():
        xbuf[0:8, :] = jnp.zeros((8, GROUP_W), F32)

    u = jax.nn.gelu(p_ref[:, 0:GROUP_W])
    vf = jax.nn.gelu(p_ref[:, GROUP_W:2 * GROUP_W])
    mean = jnp.mean(vf, -1, keepdims=True)
    var = jnp.mean(jnp.square(vf - mean), -1, keepdims=True)
    v = (vf - mean) * lax.rsqrt(var + SGU_LN_EPS) * lng_ref[...] + lnb_ref[...]
    hm = _head_masks()
    ri = lax.broadcasted_iota(jnp.int32, (SGU_CHUNK, SGU_CHUNK), 0)
    ci = lax.broadcasted_iota(jnp.int32, (SGU_CHUNK, SGU_CHUNK), 1)
    ws = [jnp.where(ri >= ci, ws_ref[h], 0.0).astype(BF16) for h in range(N_HEADS)]
    bs = bs_ref[...]
    for c in range(rows // SGU_CHUNK):
        cs = slice(c * SGU_CHUNK, (c + 1) * SGU_CHUNK)
        vc = v[cs].astype(BF16)
        mixed = bs
        for h in range(N_HEADS):
            mixed = mixed + hm[h] * jnp.dot(ws[h], vc, preferred_element_type=F32)
        o_ref[cs, 0:GROUP_W] = (u[cs] * mixed).astype(o_ref.dtype)

    xbuf[8:8 + rows, :] = p_ref[:, 3 * GROUP_W:4 * GROUP_W] * p_ref[:, 4 * GROUP_W:5 * GROUP_W]
    acc = cw_ref[2:3, :] * xbuf[8:8 + rows, :]
    for tap in range(2):
        acc = acc + cw_ref[tap:tap + 1, :] * xbuf[6 + tap:6 + tap + rows, :]
    xbuf[0:8, :] = xbuf[rows:rows + 8, :]
    o_ref[:, GROUP_W:2 * GROUP_W] = (p_ref[:, 2 * GROUP_W:3 * GROUP_W] * acc).astype(o_ref.dtype)


def _sgu_conv(psc, ln_g, ln_b, w_s, b_s, conv_w, bn, seq):
    t = psc.shape[0]
    rows = ROW_BLOCK
    nb = seq // rows
    bs_exp = jnp.repeat(b_s.T.astype(F32), HEAD_DIM, axis=1)
    consts = (ln_g[None, :].astype(F32), ln_b[None, :].astype(F32), w_s.astype(F32), bs_exp, conv_w.astype(F32))
    rowmap = lambda b, j: (b * nb + j, 0)
    return pl.pallas_call(
        _sgu_conv_kernel,
        out_shape=jax.ShapeDtypeStruct((t, 2 * GROUP_W), BF16),
        grid=(bn, nb),
        in_specs=[pl.BlockSpec((rows, 5 * GROUP_W), rowmap)]
                 + [pl.BlockSpec(c.shape, lambda b, j, n=c.ndim: (0,) * n) for c in consts],
        out_specs=pl.BlockSpec((rows, 2 * GROUP_W), rowmap),
        scratch_shapes=[pltpu.VMEM((rows + 8, GROUP_W), F32)],
        compiler_params=_cparams(("arbitrary", "arbitrary")),
        name="sgu_conv_mixer",
    )(psc, *consts)


def _rwkv_kernel(p_ref, mu_ref, w0_ref, wup_ref, a0_ref, aup_ref, gup_ref, kk_ref, ka_ref, rk_ref, gng_ref, gnb_ref,
                 seg_ref, tri_ref, full_ref,
                 o_ref,
                 prev, s_ref, at_s, bt_s, kt_s, rt_s, v_s, btl_s, ktl_s, gam_s, bon_s, gate_s, wa_s, u_s, y_s, snap_s):
    rows = p_ref.shape[0]
    j = pl.program_id(1)

    @pl.when(j == 0)
    def _():
        prev[...] = jnp.zeros(prev.shape, F32)
        s_ref[...] = jnp.zeros(s_ref.shape, F32)

    prev[8:8 + rows, :] = p_ref[...]
    p = p_ref[...]
    p = p + (prev[7:7 + rows, :] - p) * mu_ref[...]
    prev[0:8, :] = prev[rows:rows + 8, :]
    g_w = GROUP_W
    r = p[:, 0:g_w]
    k = p[:, g_w:2 * g_w]
    v = p[:, 2 * g_w:3 * g_w]
    o = 3 * g_w
    xw = p[:, o:o + RW_LORA_W]
    xa = p[:, o + RW_LORA_W:o + RW_LORA_W + RW_LORA_A]
    xg = p[:, o + RW_LORA_W + RW_LORA_A:o + RW_LORA_W + RW_LORA_A + RW_LORA_G]
    w_log = -_softplus(-(w0_ref[...] + _dot(jnp.tanh(xw), wup_ref[...]))) - 0.5
    lw = -jnp.exp(w_log)
    a = _sigmoid(a0_ref[...] + _dot(xa, aup_ref[...]))
    gate_s[...] = _dot(_sigmoid(xg), gup_ref[...])
    seg = seg_ref[...]
    kk = k * kk_ref[...]
    kk = kk * lax.rsqrt(_seg_sum(kk * kk, seg) + 1e-12)
    k_mod = k * (1.0 + (a - 1.0) * ka_ref[...])
    bon_s[...] = _seg_sum(r * k_mod * rk_ref[...], seg) * v
    cl = _dot_exact_x(tri_ref[...], lw)
    ct = _dot_exact_x(full_ref[...], lw)
    e_neg = jnp.exp(-cl)
    e_tail = jnp.exp(ct - cl)
    zb = kk * a
    at_s[...] = -kk * jnp.exp(cl - lw)
    bt_s[...] = zb * e_neg
    kt_s[...] = k_mod * e_neg
    rt_s[...] = r * jnp.exp(cl)
    v_s[...] = v
    btl_s[...] = zb * e_tail
    ktl_s[...] = k_mod * e_tail
    gam_s[...] = jnp.exp(ct)

    hm = _head_masks()
    strict, incl, eye = _tile_masks()
    bd = _block_diag_mask()
    gng, gnb = gng_ref[...], gnb_ref[...]

    tiles = range(rows // TILE)
    heads = range(N_HEADS)
    chains = [(t, h) for t in tiles for h in heads]
    rsl = [slice(t * TILE, (t + 1) * TILE) for t in tiles]

    x, lak, mrb, mrk = {}, {}, {}, {}
    for t in tiles:
        at, rt = at_s[rsl[t], :], rt_s[rsl[t], :]
        rhs_nt = jnp.concatenate([bt_s[rsl[t], :], kt_s[rsl[t], :]], axis=0).astype(BF16)
        for h in heads:
            sc = _dot_nt(jnp.concatenate([at * hm[h], rt * hm[h]], axis=0), rhs_nt)
            x[t, h] = jnp.where(strict, -sc[0:TILE, 0:TILE], 0.0)
            lak[t, h] = jnp.where(strict, sc[0:TILE, TILE:2 * TILE], 0.0).astype(BF16)
            mrb[t, h] = jnp.where(incl, sc[TILE:2 * TILE, 0:TILE], 0.0).astype(BF16)
            mrk[t, h] = jnp.where(incl, sc[TILE:2 * TILE, TILE:2 * TILE], 0.0).astype(BF16)
    p = _unit_lower_inverses(x, eye, chains)
    for t in tiles:
        at = at_s[rsl[t], :]
        vt = v_s[rsl[t], :].astype(BF16)
        wa = jnp.zeros((TILE, GROUP_W), F32)
        u0 = jnp.zeros((TILE, GROUP_W), F32)
        y0 = jnp.zeros((TILE, GROUP_W), F32)
        for h in heads:
            sol = _dot(p[t, h], jnp.concatenate([at, _dot(lak[t, h], vt)], axis=1))
            wa = wa + hm[h] * sol[:, 0:GROUP_W]
            u0 = u0 + hm[h] * sol[:, GROUP_W:2 * GROUP_W]
            y0 = y0 + hm[h] * _dot(mrk[t, h], vt)
        wa_s[rsl[t], :] = wa
        u_s[rsl[t], :] = u0
        y_s[rsl[t], :] = y0

    chunks = range(rows // CHUNK)
    csl = [slice(c * CHUNK, (c + 1) * CHUNK) for c in chunks]
    pmat = [(bd * _dot_tn(wa_s[csl[c], :], btl_s[csl[c], :])).astype(BF16) for c in chunks]
    qmat = [bd * _dot_tn(jnp.concatenate([u_s[csl[c], :], v_s[csl[c], :]], axis=0),
                         jnp.concatenate([btl_s[csl[c], :], ktl_s[csl[c], :]], axis=0)) for c in chunks]
    s = s_ref[...]
    for c in chunks:
        snap_s[c] = s.astype(BF16)
        s = s * gam_s[c * CHUNK:c * CHUNK + 1, :] + _dot(s, pmat[c]) + qmat[c]
    s_ref[...] = s
    for c in chunks:
        us = _dot_nt(jnp.concatenate([wa_s[csl[c], :], rt_s[csl[c], :]], axis=0), snap_s[c])
        u_s[csl[c], :] = u_s[csl[c], :] + us[0:CHUNK]
        y_s[csl[c], :] = y_s[csl[c], :] + us[CHUNK:2 * CHUNK]

    for t in tiles:
        u = u_s[rsl[t], :].astype(BF16)
        y = y_s[rsl[t], :]
        for h in heads:
            y = y + hm[h] * _dot(mrb[t, h], u)
        mean = _seg_sum(y, seg) * (1.0 / HEAD_DIM)
        yc = y - mean
        var = _seg_sum(yc * yc, seg) * (1.0 / HEAD_DIM)
        yn = yc * lax.rsqrt(var + RW_GN_EPS) * gng + gnb
        o_ref[rsl[t], :] = ((yn + bon_s[rsl[t], :]) * gate_s[rsl[t], :]).astype(o_ref.dtype)


def _rwkv(prw, mu, w0, w_up, a0, a_up, g_up, k_k, k_a, r_k, gn_g, gn_b, bn, seq):
    t = prw.shape[0]
    rows = ROW_BLOCK
    nb = seq // rows
    _, _, _, seg = _expand_mats()
    tri, _, full = _chunk_mats(rows)
    row = lambda x: x.reshape(1, -1).astype(F32)
    consts = (row(mu), row(w0), w_up.astype(BF16), row(a0), a_up.astype(BF16), g_up.astype(BF16),
              row(k_k), row(k_a), row(r_k), row(gn_g), row(gn_b), seg, tri, full)
    rowmap = lambda b, j: (b * nb + j, 0)
    big = lambda: pltpu.VMEM((rows, GROUP_W), F32)
    return pl.pallas_call(
        _rwkv_kernel,
        out_shape=jax.ShapeDtypeStruct((t, GROUP_W), BF16),
        grid=(bn, nb),
        in_specs=[pl.BlockSpec((rows, 4 * GROUP_W), rowmap)]
                 + [pl.BlockSpec(c.shape, lambda b, j: (0, 0)) for c in consts],
        out_specs=pl.BlockSpec((rows, GROUP_W), rowmap),
        scratch_shapes=[pltpu.VMEM((rows + 8, 4 * GROUP_W), F32), pltpu.VMEM((GROUP_W, GROUP_W), F32)]
                       + [big() for _ in range(13)] + [pltpu.VMEM((rows // CHUNK, GROUP_W, GROUP_W), BF16)],
        compiler_params=_cparams(("arbitrary", "arbitrary")),
        name="rwkv7_mixer",
    )(prw, *consts)


def _out_router_kernel(x_ref, oa_ref, obc_ref, od_ref, wo_ref, gt_ref, g_ref, sh_ref, sc_ref,
                       wrh_ref, wrl_ref, br_ref, tri_ref,
                       xo_ref, hf_ref, route_ref, cnt_ref, carry):
    i = pl.program_id(0)

    @pl.when(i == 0)
    def _():
        carry[...] = jnp.zeros(carry.shape, F32)

    g_w = GROUP_W
    mixed = jnp.dot(oa_ref[...], wo_ref[0:g_w, :], preferred_element_type=F32)
    mixed += jnp.dot(obc_ref[...], wo_ref[g_w:3 * g_w, :], preferred_element_type=F32)
    mixed += jnp.dot(od_ref[...], wo_ref[3 * g_w:4 * g_w, :], preferred_element_type=F32)
    x = x_ref[...] + gt_ref[0] * mixed
    xo_ref[...] = x
    y = x * lax.rsqrt(jnp.mean(x * x, -1, keepdims=True) + NORM_EPS) * g_ref[...]
    hf = y * (1.0 + sc_ref[0]) + sh_ref[0]
    d_model = hf.shape[1]
    hf_ref[:, 0:d_model] = hf
    hh, hl = _split2(hf)
    lg = (jnp.dot(hh, wrh_ref[...], preferred_element_type=F32) + jnp.dot(hl, wrh_ref[...], preferred_element_type=F32)
          + jnp.dot(hh, wrl_ref[...], preferred_element_type=F32) + br_ref[...])
    tm = lg.shape[0]
    lane = lax.broadcasted_iota(jnp.int32, (tm, LANES), 1)
    lanef = lane.astype(F32)
    big = jnp.float32(1e9)
    ninf = jnp.float32(-jnp.inf)
    is_g = (lane >= N_EXPERTS) & (lane < N_EXPERTS + N_GROUPS)
    gl = jnp.where(is_g, lg, ninf)
    gmax = jnp.max(gl, -1, keepdims=True)
    gsel = jnp.min(jnp.where(gl == gmax, lanef - N_EXPERTS, big), -1, keepdims=True)
    p_group = 1.0 / jnp.sum(jnp.where(is_g, jnp.exp(gl - gmax), 0.0), -1, keepdims=True)
    in_grp = (lane < N_EXPERTS) & ((lane >> 3).astype(F32) == gsel)
    el = jnp.where(in_grp, lg, ninf)
    v1 = jnp.max(el, -1, keepdims=True)
    i1 = jnp.min(jnp.where(el == v1, lanef, big), -1, keepdims=True)
    el2 = jnp.where(lanef == i1, ninf, el)
    v2 = jnp.max(el2, -1, keepdims=True)
    i2 = jnp.min(jnp.where(el2 == v2, lanef, big), -1, keepdims=True)
    e21 = jnp.exp(v2 - v1)
    g1 = p_group / (1.0 + e21)
    g2 = p_group * e21 / (1.0 + e21)
    first_lo = i1 < i2
    a = jnp.where(first_lo, i1, i2) - gsel * EXPERTS_PER_GROUP
    b = jnp.where(first_lo, i2, i1) - gsel * EXPERTS_PER_GROUP
    bucket = gsel * PAIRS_PER_GROUP + a * (2 * EXPERTS_PER_GROUP - 1 - a) * 0.5 + (b - a - 1.0)
    g_lo = jnp.where(first_lo, g1, g2)
    g_hi = jnp.where(first_lo, g2, g1)
    hf_ref[:, d_model:d_model + LANES] = jnp.where(lane == 0, g_lo, jnp.where(lane == 1, g_hi, 0.0))
    lane2 = lax.broadcasted_iota(jnp.int32, (tm, 2 * LANES), 1).astype(F32)
    oh = lane2 == bucket
    total = jnp.dot(tri_ref[...], jnp.where(oh, 1.0, 0.0).astype(BF16), preferred_element_type=F32) + carry[...]
    rank = jnp.sum(jnp.where(oh, total, 0.0), -1, keepdims=True)
    carry[...] = carry[...] + jnp.sum(jnp.where(oh, 1.0, 0.0), axis=0, keepdims=True)
    cnt_ref[...] = carry[...]
    route_ref[...] = jnp.where(lane == 0, bucket, jnp.where(lane == 1, rank, 0.0))


def _out_router(x, oa, obc, od, w_out, gt, g, shift, scale, wr_hi, wr_lo, b_r, seq):
    t, d = x.shape
    tm = ROW_BLOCK
    per_b = seq // tm
    tri = (jnp.arange(tm)[:, None] > jnp.arange(tm)[None, :]).astype(BF16)
    bspec = pl.BlockSpec((1, 1, d), lambda i: (i // per_b, 0, 0))
    full = lambda a: pl.BlockSpec(a.shape, lambda i: (0,) * a.ndim)
    return pl.pallas_call(
        _out_router_kernel,
        out_shape=(jax.ShapeDtypeStruct((t, d), F32), jax.ShapeDtypeStruct((t, d + LANES), F32),
                   jax.ShapeDtypeStruct((t, LANES), F32), jax.ShapeDtypeStruct((1, 2 * LANES), F32)),
        grid=(t // tm,),
        in_specs=[pl.BlockSpec((tm, d), lambda i: (i, 0)),
                  pl.BlockSpec((tm, GROUP_W), lambda i: (i, 0)),
                  pl.BlockSpec((tm, 2 * GROUP_W), lambda i: (i, 0)),
                  pl.BlockSpec((tm, GROUP_W), lambda i: (i, 0)),
                  full(w_out), bspec, full(g), bspec, bspec, full(wr_hi), full(wr_lo), full(b_r), full(tri)],
        out_specs=(pl.BlockSpec((tm, d), lambda i: (i, 0)), pl.BlockSpec((tm, d + LANES), lambda i: (i, 0)),
                   pl.BlockSpec((tm, LANES), lambda i: (i, 0)), pl.BlockSpec((1, 2 * LANES), lambda i: (0, 0))),
        scratch_shapes=[pltpu.VMEM((1, 2 * LANES), F32)],
        compiler_params=_cparams(("arbitrary",)),
        name="out_proj_router",
    )(x, oa, obc, od, w_out, gt, g, shift, scale, wr_hi, wr_lo, b_r, tri)


def _row_copy(src, src_row, dst, dst_row, sem):
    return pltpu.make_async_copy(src.at[pl.ds(src_row, 1), :], dst.at[pl.ds(dst_row, 1), :], sem)


DMA_UNROLL = 16


def _dest_row(ps_ref, rt_ref, r):
    return ps_ref[rt_ref[0, 0, 2 * r]] + rt_ref[0, 0, 2 * r + 1]


def _dispatch_kernel(ps_ref, rt_ref, hf_ref, xs_in_ref, xs_ref, sem):
    del xs_in_ref
    n_tok = hf_ref.shape[0]

    def issue(g, carry):
        for uu in range(DMA_UNROLL):
            r = g * DMA_UNROLL + uu
            _row_copy(hf_ref, r, xs_ref, _dest_row(ps_ref, rt_ref, r), sem).start()
        return carry

    lax.fori_loop(0, n_tok // DMA_UNROLL, issue, 0)
    pltpu.make_async_copy(hf_ref, xs_ref.at[pl.ds(0, n_tok), :], sem).wait()


def _dispatch(pad_start, rt3, hf, p_rows):
    t, d = hf.shape
    td = MOE_TOK
    xs0 = jnp.zeros((p_rows, d), F32)
    grid_spec = pltpu.PrefetchScalarGridSpec(
        num_scalar_prefetch=1,
        grid=(t // td,),
        in_specs=[pl.BlockSpec((1, 1, 2 * td), lambda i, ps: (i, 0, 0), memory_space=pltpu.SMEM),
                  pl.BlockSpec((td, d), lambda i, ps: (i, 0)),
                  pl.BlockSpec(memory_space=pl.ANY)],
        out_specs=pl.BlockSpec(memory_space=pl.ANY),
        scratch_shapes=[pltpu.SemaphoreType.DMA(())],
    )
    return pl.pallas_call(
        _dispatch_kernel,
        out_shape=jax.ShapeDtypeStruct((p_rows, d), F32),
        grid_spec=grid_spec,
        input_output_aliases={3: 0},
        compiler_params=_cparams(("arbitrary",)),
        name="moe_dispatch",
    )(pad_start, rt3, hf, xs0)


def _expert_mlp(xb, wg_b, wu_b, wd_b):
    hid = _silu(jnp.dot(xb, wg_b[...], preferred_element_type=F32)) * jnp.dot(xb, wu_b[...], preferred_element_type=F32)
    return jnp.dot(hid.astype(BF16), wd_b[...], preferred_element_type=F32)


def _expert_kernel(lo_ref, hi_ref, nu_ref, xs_ref, wg0_ref, wu0_ref, wd0_ref, wg1_ref, wu1_ref, wd1_ref, ys_ref,
                   wg0_b, wu0_b, wd0_b, wg1_b, wu1_b, wd1_b):
    i = pl.program_id(0)
    used = i < nu_ref[0]
    prev = jnp.maximum(i - 1, 0)

    @pl.when(used & ((i == 0) | (lo_ref[i] != lo_ref[prev])))
    def _():
        wg0_b[...] = wg0_ref[...].astype(BF16)
        wu0_b[...] = wu0_ref[...].astype(BF16)
        wd0_b[...] = wd0_ref[...].astype(BF16)

    @pl.when(used & ((i == 0) | (hi_ref[i] != hi_ref[prev])))
    def _():
        wg1_b[...] = wg1_ref[...].astype(BF16)
        wu1_b[...] = wu1_ref[...].astype(BF16)
        wd1_b[...] = wd1_ref[...].astype(BF16)

    @pl.when(used)
    def _():
        d = ys_ref.shape[1]
        xb = xs_ref[:, 0:d].astype(BF16)
        gates = xs_ref[:, d:d + LANES]
        ys_ref[...] = (gates[:, 0:1] * _expert_mlp(xb, wg0_b, wu0_b, wd0_b)
                       + gates[:, 1:2] * _expert_mlp(xb, wg1_b, wu1_b, wd1_b))

    @pl.when(jnp.logical_not(used))
    def _():
        ys_ref[...] = jnp.zeros(ys_ref.shape, F32)


def _experts(blk_lo, blk_hi, n_used, xs, w_gate, w_up, w_down, layer):
    p_rows, dx = xs.shape
    d = dx - LANES
    bm = MOE_ROWS
    wspec = lambda shape, which: pl.BlockSpec(
        (None, None) + shape, lambda i, lo, hi, nu: (layer, (lo, hi)[which][i], 0, 0))
    grid_spec = pltpu.PrefetchScalarGridSpec(
        num_scalar_prefetch=3,
        grid=(p_rows // bm,),
        in_specs=[pl.BlockSpec((bm, dx), lambda i, lo, hi, nu: (i, 0)),
                  wspec((d, D_EXPERT), 0), wspec((d, D_EXPERT), 0), wspec((D_EXPERT, d), 0),
                  wspec((d, D_EXPERT), 1), wspec((d, D_EXPERT), 1), wspec((D_EXPERT, d), 1)],
        out_specs=pl.BlockSpec((bm, d), lambda i, lo, hi, nu: (i, 0)),
        scratch_shapes=[pltpu.VMEM((d, D_EXPERT), BF16), pltpu.VMEM((d, D_EXPERT), BF16),
                        pltpu.VMEM((D_EXPERT, d), BF16)] * 2,
    )
    return pl.pallas_call(
        _expert_kernel,
        out_shape=jax.ShapeDtypeStruct((p_rows, d), F32),
        grid_spec=grid_spec,
        compiler_params=_cparams(("arbitrary",)),
        name="moe_experts",
    )(blk_lo, blk_hi, n_used, xs, w_gate, w_up, w_down, w_gate, w_up, w_down)


def _combine_kernel(ps_ref, rt_ref, x_ref, gt_ref, fg_ref, ys_ref, o_ref, ybuf, sem, *, final):
    n_tok = x_ref.shape[0]

    def issue(g, carry):
        for uu in range(DMA_UNROLL):
            r = g * DMA_UNROLL + uu
            _row_copy(ys_ref, _dest_row(ps_ref, rt_ref, r), ybuf, r, sem).start()
        return carry

    lax.fori_loop(0, n_tok // DMA_UNROLL, issue, 0)
    pltpu.make_async_copy(ys_ref.at[pl.ds(0, n_tok), :], ybuf, sem).wait()
    x = x_ref[...] + gt_ref[0] * ybuf[...]
    if final:
        x = x * lax.rsqrt(jnp.mean(x * x, -1, keepdims=True) + NORM_EPS) * fg_ref[...]
    o_ref[...] = x


def _combine(pad_start, rt3, x, gt, final_g, ys, seq, final):
    t, d = x.shape
    tc = MOE_TOK
    per_b = seq // tc
    grid_spec = pltpu.PrefetchScalarGridSpec(
        num_scalar_prefetch=1,
        grid=(t // tc,),
        in_specs=[pl.BlockSpec((1, 1, 2 * tc), lambda i, ps: (i, 0, 0), memory_space=pltpu.SMEM),
                  pl.BlockSpec((tc, d), lambda i, ps: (i, 0)),
                  pl.BlockSpec((1, 1, d), lambda i, ps: (i // per_b, 0, 0)),
                  pl.BlockSpec((1, d), lambda i, ps: (0, 0)),
                  pl.BlockSpec(memory_space=pl.ANY)],
        out_specs=pl.BlockSpec((tc, d), lambda i, ps: (i, 0)),
        scratch_shapes=[pltpu.VMEM((tc, d), F32), pltpu.SemaphoreType.DMA(())],
    )
    return pl.pallas_call(
        functools.partial(_combine_kernel, final=final),
        out_shape=jax.ShapeDtypeStruct((t, d), F32),
        grid_spec=grid_spec,
        compiler_params=_cparams(("arbitrary",)),
        name="moe_combine",
    )(pad_start, rt3, x, gt, final_g, ys)


def _bucket_experts():
    lo, hi = [], []
    for g in range(N_GROUPS):
        for a in range(EXPERTS_PER_GROUP):
            for b in range(a + 1, EXPERTS_PER_GROUP):
                lo.append(g * EXPERTS_PER_GROUP + a)
                hi.append(g * EXPERTS_PER_GROUP + b)
    return jnp.asarray(lo, jnp.int32), jnp.asarray(hi, jnp.int32)


def _route_plan(route, counts, t):
    bm = MOE_ROWS
    cnt = counts[0, :N_BUCKETS].astype(jnp.int32)
    padded = (cnt + bm - 1) // bm * bm
    pad_end = jnp.cumsum(padded)
    pad_start = pad_end - padded
    p_rows = t + N_BUCKETS * bm
    n_blk = p_rows // bm
    blk_start = jnp.arange(n_blk, dtype=jnp.int32) * bm
    blk_b = jnp.minimum(jnp.sum((pad_end[None, :] <= blk_start[:, None]).astype(jnp.int32), axis=1), N_BUCKETS - 1)
    lo_tab, hi_tab = _bucket_experts()
    n_used = (pad_end[-1:] // bm).astype(jnp.int32)
    rt3 = route[:, 0:2].astype(jnp.int32).reshape(t // MOE_TOK, 1, 2 * MOE_TOK)
    return pad_start, rt3, lo_tab[blk_b], hi_tab[blk_b], n_used, p_rows


def kernel(x, c, ada_w, ada_b, mix_norm_g, ffn_norm_g, w_in, w_out, gdn_conv_w, gdn_a_log, gdn_dt_bias, gdn_norm_g,
           sgu_ln_g, sgu_ln_b, sgu_w, sgu_b, sc_conv_w, rw_mu, rw_w0, rw_w_up, rw_a0, rw_a_up, rw_g_up, rw_k_k,
           rw_k_a, rw_r_k, rw_gn_g, rw_gn_b, moe_w_group, moe_b_group, moe_w_router, moe_b_router, moe_w_gate,
           moe_w_up, moe_w_down, final_norm_g):
    bn, seq, d = x.shape
    depth = ada_w.shape[0]
    t = bn * seq
    assert d == 4 * GROUP_W and seq % ROW_BLOCK == 0 and ROW_BLOCK % MOE_TOK == 0
    g_w = GROUP_W
    mod = _ada(c, ada_w, ada_b)
    xf = x.reshape(t, d)
    o_z, o_a, o_su = 3 * g_w, 4 * g_w, 4 * g_w + 2 * N_HEADS
    o_rp = o_su + 5 * g_w
    for l in range(depth):
        m = mod[l].reshape(bn, 6, 1, d)
        sh_m, sc_m, gt_m, sh_f, sc_f, gt_f = (m[:, i] for i in range(6))
        wl = w_in[l]
        w_ab = wl[:, o_a:o_su]
        w_r = jnp.concatenate([wl[:, 0:o_a], wl[:, o_su:o_rp], wl[:, o_rp:],
                               jnp.pad(w_ab, ((0, 0), (0, LANES - 2 * N_HEADS)))], axis=1).astype(BF16)
        pg, psc, prw, pab, abt = _in_proj(xf, mix_norm_g[l][None, :], sh_m, sc_m, w_r, w_ab.T.astype(BF16), seq)
        oa = _gdn(pg, pab, abt, gdn_conv_w[l], gdn_a_log[l], gdn_dt_bias[l], gdn_norm_g[l], bn, seq)
        obc = _sgu_conv(psc, sgu_ln_g[l], sgu_ln_b[l], sgu_w[l], sgu_b[l], sc_conv_w[l], bn, seq)
        od = _rwkv(prw, rw_mu[l], rw_w0[l], rw_w_up[l], rw_a0[l], rw_a_up[l], rw_g_up[l], rw_k_k[l], rw_k_a[l],
                   rw_r_k[l], rw_gn_g[l], rw_gn_b[l], bn, seq)
        w_rt = jnp.concatenate([moe_w_router[l], moe_w_group[l],
                                jnp.zeros((d, LANES - N_EXPERTS - N_GROUPS), F32)], axis=1)
        wr_hi = w_rt.astype(BF16)
        wr_lo = (w_rt - wr_hi.astype(F32)).astype(BF16)
        b_r = jnp.concatenate([moe_b_router[l], moe_b_group[l], jnp.zeros((LANES - N_EXPERTS - N_GROUPS,), F32)])[None, :]
        xf, hf, route, counts = _out_router(xf, oa, obc, od, w_out[l].astype(BF16), gt_m, ffn_norm_g[l][None, :],
                                            sh_f, sc_f, wr_hi, wr_lo, b_r, seq)
        pad_start, rt3, blk_lo, blk_hi, n_used, p_rows = _route_plan(route, counts, t)
        xs = _dispatch(pad_start, rt3, hf, p_rows)
        ys = _experts(blk_lo, blk_hi, n_used, xs, moe_w_gate, moe_w_up, moe_w_down, l)
        xf = _combine(pad_start, rt3, xf, gt_f, final_norm_g[None, :], ys, seq, final=(l == depth - 1))
    return xf.reshape(bn, seq, d)
```

```python
import functools
import math

import jax
import jax.numpy as jnp
from jax import lax
from jax.experimental import pallas as pl
from jax.experimental.pallas import tpu as pltpu

F32 = jnp.float32
BF16 = jnp.bfloat16

HEAD_DIM = 64
N_HEADS = 4
GROUP_W = HEAD_DIM * N_HEADS
CHUNK = 64
TILE = 2 * CHUNK
NORM_EPS = 1e-6
SGU_CHUNK = 128
SGU_LN_EPS = 1e-5
RW_GN_EPS = 64e-5
RW_LORA_W, RW_LORA_A, RW_LORA_G = 64, 64, 128
N_GROUPS = 8
EXPERTS_PER_GROUP = 8
N_EXPERTS = N_GROUPS * EXPERTS_PER_GROUP
D_EXPERT = 256
PAIRS_PER_GROUP = EXPERTS_PER_GROUP * (EXPERTS_PER_GROUP - 1) // 2
N_BUCKETS = N_GROUPS * PAIRS_PER_GROUP
LANES = 128
VMEM_LIMIT = 56 * 1024 * 1024

ROW_BLOCK = 512
MOE_ROWS = 128
MOE_TOK = 512


def _dot(a, b):
    return jnp.dot(a.astype(BF16), b.astype(BF16), preferred_element_type=F32)


def _dot_nt(a, b):
    return lax.dot_general(a.astype(BF16), b.astype(BF16), (((1,), (1,)), ((), ())),
                           preferred_element_type=F32)


def _dot_tn(a, b):
    return lax.dot_general(a.astype(BF16), b.astype(BF16), (((0,), (0,)), ((), ())),
                           preferred_element_type=F32)


def _split2(x):
    hi = x.astype(BF16)
    lo = (x - hi.astype(F32)).astype(BF16)
    return hi, lo


def _dot_x_exact(x, m):
    hi, lo = _split2(x)
    return (jnp.dot(hi, m, preferred_element_type=F32) + jnp.dot(lo, m, preferred_element_type=F32))


def _dot_exact_x(m, x):
    hi, lo = _split2(x)
    return (jnp.dot(m, hi, preferred_element_type=F32) + jnp.dot(m, lo, preferred_element_type=F32))


def _sigmoid(x):
    return 1.0 / (1.0 + jnp.exp(-x))


def _silu(x):
    return x * _sigmoid(x)


def _softplus(x):
    return jnp.maximum(x, 0.0) + jnp.log(1.0 + jnp.exp(-jnp.abs(x)))


def _head_masks():
    lane = lax.broadcasted_iota(jnp.int32, (1, GROUP_W), 1)
    return [((lane >> 6) == h).astype(F32) for h in range(N_HEADS)]


def _tile_masks():
    ri = lax.broadcasted_iota(jnp.int32, (TILE, TILE), 0)
    ci = lax.broadcasted_iota(jnp.int32, (TILE, TILE), 1)
    same = (ri >> 6) == (ci >> 6)
    return same & (ri > ci), same & (ri >= ci), (ri == ci).astype(F32)


def _block_diag_mask():
    ri = lax.broadcasted_iota(jnp.int32, (GROUP_W, GROUP_W), 0)
    ci = lax.broadcasted_iota(jnp.int32, (GROUP_W, GROUP_W), 1)
    return ((ri >> 6) == (ci >> 6)).astype(F32)


def _merge_masks():
    ri = lax.broadcasted_iota(jnp.int32, (TILE, TILE), 0)
    ci = lax.broadcasted_iota(jnp.int32, (TILE, TILE), 1)
    return [((ri >> (l + 1)) == (ci >> (l + 1))) & (((ri >> l) & 1) == 1) & (((ci >> l) & 1) == 0) for l in range(6)]


def _unit_lower_inverses(a, eye, chains):
    masks = _merge_masks()
    d = {c: eye - jnp.where(masks[0], a[c], 0.0) for c in chains}
    for l in range(1, 6):
        f = {c: _dot(jnp.where(masks[l], a[c], 0.0), d[c]) for c in chains}
        d = {c: d[c] - _dot(d[c], f[c]) for c in chains}
    return d


def _seg_sum(x, seg):
    return jnp.dot(x.astype(BF16), seg, preferred_element_type=F32)


def _cparams(sem):
    return pltpu.CompilerParams(dimension_semantics=sem, vmem_limit_bytes=VMEM_LIMIT)


def _ada_kernel(c_ref, w_ref, b_ref, o_ref):
    c = c_ref[...]
    ca = _silu(c)
    chi, clo = _split2(ca)
    w = w_ref[...]
    whi, wlo = _split2(w)
    acc = jnp.dot(chi, whi, preferred_element_type=F32)
    acc += jnp.dot(clo, whi, preferred_element_type=F32)
    acc += jnp.dot(chi, wlo, preferred_element_type=F32)
    o_ref[...] = acc + b_ref[...]


def _ada(c, ada_w, ada_b):
    depth, d, d6 = ada_w.shape
    bn = c.shape[0]
    nj = d6 // d
    return pl.pallas_call(
        _ada_kernel,
        out_shape=jax.ShapeDtypeStruct((depth, bn, d6), F32),
        grid=(depth, nj),
        in_specs=[pl.BlockSpec((bn, d), lambda l, j: (0, 0)),
                  pl.BlockSpec((None, d, d), lambda l, j: (l, 0, j)),
                  pl.BlockSpec((None, 1, d), lambda l, j: (l, 0, j))],
        out_specs=pl.BlockSpec((None, bn, d), lambda l, j: (l, 0, j)),
        compiler_params=_cparams(("arbitrary", "arbitrary")),
        name="ada_mod",
    )(c, ada_w, ada_b.reshape(depth, 1, d6))


def _in_proj_kernel(x_ref, g_ref, sh_ref, sc_ref, w_ref, wab_ref, pg_ref, psc_ref, prw_ref, pab_ref, abt_ref):
    x = x_ref[...]
    y = x * lax.rsqrt(jnp.mean(x * x, -1, keepdims=True) + NORM_EPS) * g_ref[...]
    h = (y * (1.0 + sc_ref[0]) + sh_ref[0]).astype(BF16)
    o = 0
    for ref in (pg_ref, psc_ref, prw_ref, pab_ref):
        w = ref.shape[1]
        ref[...] = jnp.dot(h, w_ref[:, o:o + w], preferred_element_type=F32)
        o += w
    abt_ref[...] = lax.dot_general(wab_ref[...], h, (((1,), (1,)), ((), ())), preferred_element_type=F32)


def _in_proj(x, g, shift, scale, w_r, w_abt, seq):
    t, d = x.shape
    tm = ROW_BLOCK
    per_b = seq // tm
    widths = (4 * GROUP_W, 5 * GROUP_W, 4 * GROUP_W, LANES)
    bspec = pl.BlockSpec((1, 1, d), lambda i: (i // per_b, 0, 0))
    return pl.pallas_call(
        _in_proj_kernel,
        out_shape=tuple(jax.ShapeDtypeStruct((t, w), F32) for w in widths) + (jax.ShapeDtypeStruct((8, t), F32),),
        grid=(t // tm,),
        in_specs=[pl.BlockSpec((tm, d), lambda i: (i, 0)),
                  pl.BlockSpec((1, d), lambda i: (0, 0)),
                  bspec, bspec,
                  pl.BlockSpec(w_r.shape, lambda i: (0, 0)),
                  pl.BlockSpec(w_abt.shape, lambda i: (0, 0))],
        out_specs=tuple(pl.BlockSpec((tm, w), lambda i: (i, 0)) for w in widths) + (pl.BlockSpec((8, tm), lambda i: (0, i)),),
        compiler_params=_cparams(("arbitrary",)),
        name="in_proj",
    )(x, g, shift, scale, w_r, w_abt)


def _gdn_kernel(p_ref, ab_ref, abt_ref, cw_ref, alog_ref, dtb_ref, alogt_ref, dtbt_ref, ng_ref,
                eg_ref, eb_ref, seg_ref, tri_ref, trit_ref, full_ref,
                o_ref,
                xbuf, s_ref, q_s, k_s, kb_s, rhs_s, qd_s, kt_s, cd_s, gc4_s, gct_s, u_s, w_s, snap_s):
    rows = p_ref.shape[0]
    j = pl.program_id(1)

    @pl.when(j == 0)
    def _():
        xbuf[0:8, :] = jnp.zeros((8, xbuf.shape[1]), F32)
        s_ref[...] = jnp.zeros(s_ref.shape, F32)

    xbuf[8:8 + rows, :] = p_ref[:, 0:3 * GROUP_W]
    acc = cw_ref[3:4, :] * xbuf[8:8 + rows, :]
    for tap in range(3):
        acc = acc + cw_ref[tap:tap + 1, :] * xbuf[5 + tap:5 + tap + rows, :]
    xbuf[0:8, :] = xbuf[rows:rows + 8, :]
    qkv = _silu(acc)
    seg = seg_ref[...]
    q = qkv[:, 0:GROUP_W]
    k = qkv[:, GROUP_W:2 * GROUP_W]
    v = qkv[:, 2 * GROUP_W:3 * GROUP_W]
    q = q * lax.rsqrt(_seg_sum(q * q, seg) + 1e-6) * (HEAD_DIM ** -0.5)
    k = k * lax.rsqrt(_seg_sum(k * k, seg) + 1e-6)

    ab = ab_ref[...]
    g = -jnp.exp(alog_ref[...]) * _softplus(ab + dtb_ref[...])
    beta = jnp.dot(_sigmoid(ab).astype(BF16), eb_ref[...], preferred_element_type=F32)
    tsl = [slice(t * TILE, (t + 1) * TILE) for t in range(rows // TILE)]
    gc4 = jnp.concatenate([_dot_exact_x(tri_ref[...], g[sl]) for sl in tsl], axis=0)
    gl4 = jnp.concatenate([_dot_exact_x(full_ref[...], g[sl]) for sl in tsl], axis=0)
    gc4_s[...] = gc4
    gc = _dot_x_exact(gc4, eg_ref[...])
    gl = _dot_x_exact(gl4, eg_ref[...])
    egc = jnp.exp(gc)
    kb = k * beta
    q_s[...] = q
    k_s[...] = k
    kb_s[...] = kb
    rhs_s[:, 0:GROUP_W] = v * beta
    rhs_s[:, GROUP_W:2 * GROUP_W] = kb * egc
    qd_s[...] = q * egc
    kt_s[...] = k * jnp.exp(gl - gc)
    cd_s[...] = jnp.exp(gl)
    abt = abt_ref[...]
    gt = -jnp.exp(alogt_ref[...]) * _softplus(abt + dtbt_ref[...])
    for t in range(rows // TILE):
        gct_s[t] = _dot_x_exact(gt[:, tsl[t]], trit_ref[...])

    hm = _head_masks()
    strict, incl, eye = _tile_masks()
    bd = _block_diag_mask()
    ng = ng_ref[...]
    tiles = range(rows // TILE)
    heads = range(N_HEADS)
    chains = [(t, h) for t in tiles for h in heads]
    rsl = [slice(t * TILE, (t + 1) * TILE) for t in tiles]

    x, attn = {}, {}
    for t in tiles:
        kt_ = k_s[rsl[t], :].astype(BF16)
        kbt, qt = kb_s[rsl[t], :], q_s[rsl[t], :]
        gct_t = gct_s[t]
        gc4_t = gc4_s[rsl[t], :]
        for h in heads:
            dec = jnp.exp(jnp.where(incl, gc4_t[:, h:h + 1] - gct_t[h:h + 1, :], -jnp.inf))
            x[t, h] = jnp.where(strict, _dot_nt(kbt * hm[h], kt_) * dec, 0.0)
            attn[t, h] = (_dot_nt(qt * hm[h], kt_) * dec).astype(BF16)
    p = _unit_lower_inverses(x, eye, chains)
    for t in tiles:
        rhs = rhs_s[rsl[t], :].astype(BF16)
        u = jnp.zeros((TILE, GROUP_W), F32)
        w = jnp.zeros((TILE, GROUP_W), F32)
        for h in heads:
            sol = _dot(p[t, h], rhs)
            u = u + hm[h] * sol[:, 0:GROUP_W]
            w = w + hm[h] * sol[:, GROUP_W:2 * GROUP_W]
        u_s[rsl[t], :] = u
        w_s[rsl[t], :] = w

    chunks = range(rows // CHUNK)
    csl = [slice(c * CHUNK, (c + 1) * CHUNK) for c in chunks]
    pq = [_dot_tn(kt_s[csl[c], :], jnp.concatenate([w_s[csl[c], :], u_s[csl[c], :]], axis=1)) for c in chunks]
    pmat = [(bd * pq[c][:, 0:GROUP_W]).astype(BF16) for c in chunks]
    s = s_ref[...]
    for c in chunks:
        snap_s[c] = s.astype(BF16)
        s = s * cd_s[c * CHUNK:c * CHUNK + 1, :] - _dot(pmat[c], s) + bd * pq[c][:, GROUP_W:2 * GROUP_W]
    s_ref[...] = s
    for c in chunks:
        ws = jnp.dot(jnp.concatenate([w_s[csl[c], :], qd_s[csl[c], :]], axis=0).astype(BF16), snap_s[c],
                     preferred_element_type=F32)
        u_s[csl[c], :] = u_s[csl[c], :] - ws[0:CHUNK]
        w_s[csl[c], :] = ws[CHUNK:2 * CHUNK]

    for t in tiles:
        vn = u_s[rsl[t], :].astype(BF16)
        o = w_s[rsl[t], :]
        for h in heads:
            o = o + hm[h] * _dot(attn[t, h], vn)
        o = o * lax.rsqrt(_seg_sum(o * o, seg) * (1.0 / HEAD_DIM) + NORM_EPS) * ng
        o_ref[rsl[t], :] = (o * _silu(p_ref[rsl[t], 3 * GROUP_W:4 * GROUP_W])).astype(o_ref.dtype)


def _chunk_mats():
    ri = jnp.arange(TILE)[:, None]
    ci = jnp.arange(TILE)[None, :]
    same = (ri // CHUNK) == (ci // CHUNK)
    tri = (same & (ci <= ri)).astype(BF16)
    return tri, tri.T, same.astype(BF16)


def _expand_mats():
    lane = jnp.arange(LANES)[:, None]
    col = jnp.arange(GROUP_W)[None, :]
    eg = (lane == col // HEAD_DIM).astype(BF16)
    eb = (lane == N_HEADS + col // HEAD_DIM).astype(BF16)
    seg = ((jnp.arange(GROUP_W)[:, None] // HEAD_DIM) == (col // HEAD_DIM)).astype(BF16)
    return eg, eb, seg


def _pad_lanes(v, n=LANES):
    return jnp.zeros((1, n), F32).at[0, :v.shape[0]].set(v.astype(F32))


def _gdn(pg, pab, abt, conv_w, a_log, dt_bias, norm_g, bn, seq):
    t = pg.shape[0]
    rows = ROW_BLOCK
    nb = seq // rows
    eg, eb, seg = _expand_mats()
    tri, trit, full = _chunk_mats()
    alog_t = jnp.zeros((8, rows), F32).at[:N_HEADS].set(jnp.broadcast_to(a_log[:, None], (N_HEADS, rows)))
    dtb_t = jnp.zeros((8, rows), F32).at[:N_HEADS].set(jnp.broadcast_to(dt_bias[:, None], (N_HEADS, rows)))
    consts = (conv_w.astype(F32), _pad_lanes(a_log), _pad_lanes(dt_bias), alog_t, dtb_t,
              jnp.tile(norm_g.astype(F32), N_HEADS)[None, :], eg, eb, seg, tri, trit, full)
    rowmap = lambda b, j: (b * nb + j, 0)
    return pl.pallas_call(
        _gdn_kernel,
        out_shape=jax.ShapeDtypeStruct((t, GROUP_W), BF16),
        grid=(bn, nb),
        in_specs=[pl.BlockSpec((rows, 4 * GROUP_W), rowmap),
                  pl.BlockSpec((rows, LANES), rowmap),
                  pl.BlockSpec((8, rows), lambda b, j: (0, b * nb + j))]
                 + [pl.BlockSpec(c.shape, lambda b, j: (0, 0)) for c in consts],
        out_specs=pl.BlockSpec((rows, GROUP_W), rowmap),
        scratch_shapes=[pltpu.VMEM((rows + 8, 3 * GROUP_W), F32),
                        pltpu.VMEM((GROUP_W, GROUP_W), F32),
                        pltpu.VMEM((rows, GROUP_W), F32), pltpu.VMEM((rows, GROUP_W), F32),
                        pltpu.VMEM((rows, GROUP_W), F32), pltpu.VMEM((rows, 2 * GROUP_W), F32),
                        pltpu.VMEM((rows, GROUP_W), F32), pltpu.VMEM((rows, GROUP_W), F32),
                        pltpu.VMEM((rows, GROUP_W), F32), pltpu.VMEM((rows, LANES), F32),
                        pltpu.VMEM((rows // TILE, 8, TILE), F32),
                        pltpu.VMEM((rows, GROUP_W), F32), pltpu.VMEM((rows, GROUP_W), F32),
                        pltpu.VMEM((rows // CHUNK, GROUP_W, GROUP_W), BF16)],
        compiler_params=_cparams(("arbitrary", "arbitrary")),
        name="gdn_mixer",
    )(pg, pab, abt, *consts)


def _sgu_conv_kernel(p_ref, lng_ref, lnb_ref, ws_ref, bs_ref, cw_ref, o_ref, xbuf):
    rows = p_ref.shape[0]
    j = pl.program_id(1)

    @pl.when(j == 0)
    def _():
        xbuf[0:8, :] = jnp.zeros((8, GROUP_W), F32)

    u = jax.nn.gelu(p_ref[:, 0:GROUP_W])
    vf = jax.nn.gelu(p_ref[:, GROUP_W:2 * GROUP_W])
    mean = jnp.mean(vf, -1, keepdims=True)
    var = jnp.mean(jnp.square(vf - mean), -1, keepdims=True)
    v = (vf - mean) * lax.rsqrt(var + SGU_LN_EPS) * lng_ref[...] + lnb_ref[...]
    hm = _head_masks()
    ri = lax.broadcasted_iota(jnp.int32, (SGU_CHUNK, SGU_CHUNK), 0)
    ci = lax.broadcasted_iota(jnp.int32, (SGU_CHUNK, SGU_CHUNK), 1)
    ws = [jnp.where(ri >= ci, ws_ref[h], 0.0).astype(BF16) for h in range(N_HEADS)]
    bs = bs_ref[...]
    for c in range(rows // SGU_CHUNK):
        cs = slice(c * SGU_CHUNK, (c + 1) * SGU_CHUNK)
        vc = v[cs].astype(BF16)
        mixed = bs
        for h in range(N_HEADS):
            mixed = mixed + hm[h] * jnp.dot(ws[h], vc, preferred_element_type=F32)
        o_ref[cs, 0:GROUP_W] = (u[cs] * mixed).astype(o_ref.dtype)

    xbuf[8:8 + rows, :] = p_ref[:, 3 * GROUP_W:4 * GROUP_W] * p_ref[:, 4 * GROUP_W:5 * GROUP_W]
    acc = cw_ref[2:3, :] * xbuf[8:8 + rows, :]
    for tap in range(2):
        acc = acc + cw_ref[tap:tap + 1, :] * xbuf[6 + tap:6 + tap + rows, :]
    xbuf[0:8, :] = xbuf[rows:rows + 8, :]
    o_ref[:, GROUP_W:2 * GROUP_W] = (p_ref[:, 2 * GROUP_W:3 * GROUP_W] * acc).astype(o_ref.dtype)


def _sgu_conv(psc, ln_g, ln_b, w_s, b_s, conv_w, bn, seq):
    t = psc.shape[0]
    rows = ROW_BLOCK
    nb = seq // rows
    bs_exp = jnp.repeat(b_s.T.astype(F32), HEAD_DIM, axis=1)
    consts = (ln_g[None, :].astype(F32), ln_b[None, :].astype(F32), w_s.astype(F32), bs_exp, conv_w.astype(F32))
    rowmap = lambda b, j: (b * nb + j, 0)
    return pl.pallas_call(
        _sgu_conv_kernel,
        out_shape=jax.ShapeDtypeStruct((t, 2 * GROUP_W), BF16),
        grid=(bn, nb),
        in_specs=[pl.BlockSpec((rows, 5 * GROUP_W), rowmap)]
                 + [pl.BlockSpec(c.shape, lambda b, j, n=c.ndim: (0,) * n) for c in consts],
        out_specs=pl.BlockSpec((rows, 2 * GROUP_W), rowmap),
        scratch_shapes=[pltpu.VMEM((rows + 8, GROUP_W), F32)],
        compiler_params=_cparams(("arbitrary", "arbitrary")),
        name="sgu_conv_mixer",
    )(psc, *consts)


def _rwkv_kernel(p_ref, mu_ref, w0_ref, wup_ref, a0_ref, aup_ref, gup_ref, kk_ref, ka_ref, rk_ref, gng_ref, gnb_ref,
                 seg_ref, tri_ref, full_ref,
                 o_ref,
                 prev, s_ref, at_s, bt_s, kt_s, rt_s, v_s, btl_s, ktl_s, gam_s, bon_s, gate_s, wa_s, u_s, y_s, snap_s):
    rows = p_ref.shape[0]
    j = pl.program_id(1)

    @pl.when(j == 0)
    def _():
        prev[...] = jnp.zeros(prev.shape, F32)
        s_ref[...] = jnp.zeros(s_ref.shape, F32)

    prev[8:8 + rows, :] = p_ref[...]
    p = p_ref[...]
    p = p + (prev[7:7 + rows, :] - p) * mu_ref[...]
    prev[0:8, :] = prev[rows:rows + 8, :]
    g_w = GROUP_W
    r = p[:, 0:g_w]
    k = p[:, g_w:2 * g_w]
    v = p[:, 2 * g_w:3 * g_w]
    o = 3 * g_w
    xw = p[:, o:o + RW_LORA_W]
    xa = p[:, o + RW_LORA_W:o + RW_LORA_W + RW_LORA_A]
    xg = p[:, o + RW_LORA_W + RW_LORA_A:o + RW_LORA_W + RW_LORA_A + RW_LORA_G]
    w_log = -_softplus(-(w0_ref[...] + _dot(jnp.tanh(xw), wup_ref[...]))) - 0.5
    lw = -jnp.exp(w_log)
    a = _sigmoid(a0_ref[...] + _dot(xa, aup_ref[...]))
    gate_s[...] = _dot(_sigmoid(xg), gup_ref[...])
    seg = seg_ref[...]
    kk = k * kk_ref[...]
    kk = kk * lax.rsqrt(_seg_sum(kk * kk, seg) + 1e-12)
    k_mod = k * (1.0 + (a - 1.0) * ka_ref[...])
    bon_s[...] = _seg_sum(r * k_mod * rk_ref[...], seg) * v
    tsl = [slice(t * TILE, (t + 1) * TILE) for t in range(rows // TILE)]
    cl = jnp.concatenate([_dot_exact_x(tri_ref[...], lw[sl]) for sl in tsl], axis=0)
    ct = jnp.concatenate([_dot_exact_x(full_ref[...], lw[sl]) for sl in tsl], axis=0)
    e_neg = jnp.exp(-cl)
    e_tail = jnp.exp(ct - cl)
    zb = kk * a
    at_s[...] = -kk * jnp.exp(cl - lw)
    bt_s[...] = zb * e_neg
    kt_s[...] = k_mod * e_neg
    rt_s[...] = r * jnp.exp(cl)
    v_s[...] = v
    btl_s[...] = zb * e_tail
    ktl_s[...] = k_mod * e_tail
    gam_s[...] = jnp.exp(ct)

    hm = _head_masks()
    strict, incl, eye = _tile_masks()
    bd = _block_diag_mask()
    gng, gnb = gng_ref[...], gnb_ref[...]

    tiles = range(rows // TILE)
    heads = range(N_HEADS)
    chains = [(t, h) for t in tiles for h in heads]
    rsl = [slice(t * TILE, (t + 1) * TILE) for t in tiles]

    x, lak, mrb, mrk = {}, {}, {}, {}
    for t in tiles:
        at, rt = at_s[rsl[t], :], rt_s[rsl[t], :]
        rhs_nt = jnp.concatenate([bt_s[rsl[t], :], kt_s[rsl[t], :]], axis=0).astype(BF16)
        for h in heads:
            sc = _dot_nt(jnp.concatenate([at * hm[h], rt * hm[h]], axis=0), rhs_nt)
            x[t, h] = jnp.where(strict, -sc[0:TILE, 0:TILE], 0.0)
            lak[t, h] = jnp.where(strict, sc[0:TILE, TILE:2 * TILE], 0.0).astype(BF16)
            mrb[t, h] = jnp.where(incl, sc[TILE:2 * TILE, 0:TILE], 0.0).astype(BF16)
            mrk[t, h] = jnp.where(incl, sc[TILE:2 * TILE, TILE:2 * TILE], 0.0).astype(BF16)
    p = _unit_lower_inverses(x, eye, chains)
    for t in tiles:
        at = at_s[rsl[t], :]
        vt = v_s[rsl[t], :].astype(BF16)
        wa = jnp.zeros((TILE, GROUP_W), F32)
        u0 = jnp.zeros((TILE, GROUP_W), F32)
        y0 = jnp.zeros((TILE, GROUP_W), F32)
        for h in heads:
            sol = _dot(p[t, h], jnp.concatenate([at, _dot(lak[t, h], vt)], axis=1))
            wa = wa + hm[h] * sol[:, 0:GROUP_W]
            u0 = u0 + hm[h] * sol[:, GROUP_W:2 * GROUP_W]
            y0 = y0 + hm[h] * _dot(mrk[t, h], vt)
        wa_s[rsl[t], :] = wa
        u_s[rsl[t], :] = u0
        y_s[rsl[t], :] = y0

    chunks = range(rows // CHUNK)
    csl = [slice(c * CHUNK, (c + 1) * CHUNK) for c in chunks]
    pmat = [(bd * _dot_tn(wa_s[csl[c], :], btl_s[csl[c], :])).astype(BF16) for c in chunks]
    qmat = [bd * _dot_tn(jnp.concatenate([u_s[csl[c], :], v_s[csl[c], :]], axis=0),
                         jnp.concatenate([btl_s[csl[c], :], ktl_s[csl[c], :]], axis=0)) for c in chunks]
    s = s_ref[...]
    for c in chunks:
        snap_s[c] = s.astype(BF16)
        s = s * gam_s[c * CHUNK:c * CHUNK + 1, :] + _dot(s, pmat[c]) + qmat[c]
    s_ref[...] = s
    for c in chunks:
        us = _dot_nt(jnp.concatenate([wa_s[csl[c], :], rt_s[csl[c], :]], axis=0), snap_s[c])
        u_s[csl[c], :] = u_s[csl[c], :] + us[0:CHUNK]
        y_s[csl[c], :] = y_s[csl[c], :] + us[CHUNK:2 * CHUNK]

    for t in tiles:
        u = u_s[rsl[t], :].astype(BF16)
        y = y_s[rsl[t], :]
        for h in heads:
            y = y + hm[h] * _dot(mrb[t, h], u)
        mean = _seg_sum(y, seg) * (1.0 / HEAD_DIM)
        yc = y - mean
        var = _seg_sum(yc * yc, seg) * (1.0 / HEAD_DIM)
        yn = yc * lax.rsqrt(var + RW_GN_EPS) * gng + gnb
        o_ref[rsl[t], :] = ((yn + bon_s[rsl[t], :]) * gate_s[rsl[t], :]).astype(o_ref.dtype)


def _rwkv(prw, mu, w0, w_up, a0, a_up, g_up, k_k, k_a, r_k, gn_g, gn_b, bn, seq):
    t = prw.shape[0]
    rows = ROW_BLOCK
    nb = seq // rows
    _, _, seg = _expand_mats()
    tri, _, full = _chunk_mats()
    row = lambda x: x.reshape(1, -1).astype(F32)
    consts = (row(mu), row(w0), w_up.astype(BF16), row(a0), a_up.astype(BF16), g_up.astype(BF16),
              row(k_k), row(k_a), row(r_k), row(gn_g), row(gn_b), seg, tri, full)
    rowmap = lambda b, j: (b * nb + j, 0)
    big = lambda: pltpu.VMEM((rows, GROUP_W), F32)
    return pl.pallas_call(
        _rwkv_kernel,
        out_shape=jax.ShapeDtypeStruct((t, GROUP_W), BF16),
        grid=(bn, nb),
        in_specs=[pl.BlockSpec((rows, 4 * GROUP_W), rowmap)]
                 + [pl.BlockSpec(c.shape, lambda b, j: (0, 0)) for c in consts],
        out_specs=pl.BlockSpec((rows, GROUP_W), rowmap),
        scratch_shapes=[pltpu.VMEM((rows + 8, 4 * GROUP_W), F32), pltpu.VMEM((GROUP_W, GROUP_W), F32)]
                       + [big() for _ in range(13)] + [pltpu.VMEM((rows // CHUNK, GROUP_W, GROUP_W), BF16)],
        compiler_params=_cparams(("arbitrary", "arbitrary")),
        name="rwkv7_mixer",
    )(prw, *consts)


def _out_router_kernel(x_ref, oa_ref, obc_ref, od_ref, wo_ref, gt_ref, g_ref, sh_ref, sc_ref,
                       wrh_ref, wrl_ref, br_ref, tri_ref,
                       xo_ref, hf_ref, route_ref, cnt_ref, carry):
    i = pl.program_id(0)

    @pl.when(i == 0)
    def _():
        carry[...] = jnp.zeros(carry.shape, F32)

    g_w = GROUP_W
    mixed = jnp.dot(oa_ref[...], wo_ref[0:g_w, :], preferred_element_type=F32)
    mixed += jnp.dot(obc_ref[...], wo_ref[g_w:3 * g_w, :], preferred_element_type=F32)
    mixed += jnp.dot(od_ref[...], wo_ref[3 * g_w:4 * g_w, :], preferred_element_type=F32)
    x = x_ref[...] + gt_ref[0] * mixed
    xo_ref[...] = x
    y = x * lax.rsqrt(jnp.mean(x * x, -1, keepdims=True) + NORM_EPS) * g_ref[...]
    hf = y * (1.0 + sc_ref[0]) + sh_ref[0]
    d_model = hf.shape[1]
    hf_ref[:, 0:d_model] = hf
    hh, hl = _split2(hf)
    lg = (jnp.dot(hh, wrh_ref[...], preferred_element_type=F32) + jnp.dot(hl, wrh_ref[...], preferred_element_type=F32)
          + jnp.dot(hh, wrl_ref[...], preferred_element_type=F32) + br_ref[...])
    tm = lg.shape[0]
    lane = lax.broadcasted_iota(jnp.int32, (tm, LANES), 1)
    lanef = lane.astype(F32)
    big = jnp.float32(1e9)
    ninf = jnp.float32(-jnp.inf)
    is_g = (lane >= N_EXPERTS) & (lane < N_EXPERTS + N_GROUPS)
    gl = jnp.where(is_g, lg, ninf)
    gmax = jnp.max(gl, -1, keepdims=True)
    gsel = jnp.min(jnp.where(gl == gmax, lanef - N_EXPERTS, big), -1, keepdims=True)
    p_group = 1.0 / jnp.sum(jnp.where(is_g, jnp.exp(gl - gmax), 0.0), -1, keepdims=True)
    in_grp = (lane < N_EXPERTS) & ((lane >> 3).astype(F32) == gsel)
    el = jnp.where(in_grp, lg, ninf)
    v1 = jnp.max(el, -1, keepdims=True)
    i1 = jnp.min(jnp.where(el == v1, lanef, big), -1, keepdims=True)
    el2 = jnp.where(lanef == i1, ninf, el)
    v2 = jnp.max(el2, -1, keepdims=True)
    i2 = jnp.min(jnp.where(el2 == v2, lanef, big), -1, keepdims=True)
    e21 = jnp.exp(v2 - v1)
    g1 = p_group / (1.0 + e21)
    g2 = p_group * e21 / (1.0 + e21)
    first_lo = i1 < i2
    a = jnp.where(first_lo, i1, i2) - gsel * EXPERTS_PER_GROUP
    b = jnp.where(first_lo, i2, i1) - gsel * EXPERTS_PER_GROUP
    bucket = gsel * PAIRS_PER_GROUP + a * (2 * EXPERTS_PER_GROUP - 1 - a) * 0.5 + (b - a - 1.0)
    g_lo = jnp.where(first_lo, g1, g2)
    g_hi = jnp.where(first_lo, g2, g1)
    hf_ref[:, d_model:d_model + LANES] = jnp.where(lane == 0, g_lo, jnp.where(lane == 1, g_hi, 0.0))
    lane2 = lax.broadcasted_iota(jnp.int32, (tm, 2 * LANES), 1).astype(F32)
    oh = lane2 == bucket
    total = jnp.dot(tri_ref[...], jnp.where(oh, 1.0, 0.0).astype(BF16), preferred_element_type=F32) + carry[...]
    rank = jnp.sum(jnp.where(oh, total, 0.0), -1, keepdims=True)
    carry[...] = carry[...] + jnp.sum(jnp.where(oh, 1.0, 0.0), axis=0, keepdims=True)
    cnt_ref[...] = carry[...]
    route_ref[...] = jnp.where(lane == 0, bucket, jnp.where(lane == 1, rank, 0.0))


def _out_router(x, oa, obc, od, w_out, gt, g, shift, scale, wr_hi, wr_lo, b_r, seq):
    t, d = x.shape
    tm = ROW_BLOCK
    per_b = seq // tm
    tri = (jnp.arange(tm)[:, None] > jnp.arange(tm)[None, :]).astype(BF16)
    bspec = pl.BlockSpec((1, 1, d), lambda i: (i // per_b, 0, 0))
    full = lambda a: pl.BlockSpec(a.shape, lambda i: (0,) * a.ndim)
    return pl.pallas_call(
        _out_router_kernel,
        out_shape=(jax.ShapeDtypeStruct((t, d), F32), jax.ShapeDtypeStruct((t, d + LANES), F32),
                   jax.ShapeDtypeStruct((t, LANES), F32), jax.ShapeDtypeStruct((1, 2 * LANES), F32)),
        grid=(t // tm,),
        in_specs=[pl.BlockSpec((tm, d), lambda i: (i, 0)),
                  pl.BlockSpec((tm, GROUP_W), lambda i: (i, 0)),
                  pl.BlockSpec((tm, 2 * GROUP_W), lambda i: (i, 0)),
                  pl.BlockSpec((tm, GROUP_W), lambda i: (i, 0)),
                  full(w_out), bspec, full(g), bspec, bspec, full(wr_hi), full(wr_lo), full(b_r), full(tri)],
        out_specs=(pl.BlockSpec((tm, d), lambda i: (i, 0)), pl.BlockSpec((tm, d + LANES), lambda i: (i, 0)),
                   pl.BlockSpec((tm, LANES), lambda i: (i, 0)), pl.BlockSpec((1, 2 * LANES), lambda i: (0, 0))),
        scratch_shapes=[pltpu.VMEM((1, 2 * LANES), F32)],
        compiler_params=_cparams(("arbitrary",)),
        name="out_proj_router",
    )(x, oa, obc, od, w_out, gt, g, shift, scale, wr_hi, wr_lo, b_r, tri)


def _row_copy(src, src_row, dst, dst_row, sem):
    return pltpu.make_async_copy(src.at[pl.ds(src_row, 1), :], dst.at[pl.ds(dst_row, 1), :], sem)


DMA_UNROLL = 16


def _dest_row(ps_ref, rt_ref, r):
    return ps_ref[rt_ref[0, 0, 2 * r]] + rt_ref[0, 0, 2 * r + 1]


def _dispatch_kernel(ps_ref, rt_ref, hf_ref, xs_in_ref, xs_ref, sem):
    del xs_in_ref
    i = pl.program_id(0)
    n_tok = rt_ref.shape[2] // 2
    base = i * n_tok

    def issue(g, carry):
        for uu in range(DMA_UNROLL):
            r = g * DMA_UNROLL + uu
            _row_copy(hf_ref, base + r, xs_ref, _dest_row(ps_ref, rt_ref, r), sem).start()
        return carry

    lax.fori_loop(0, n_tok // DMA_UNROLL, issue, 0)

    def wait_step():
        pltpu.make_async_copy(hf_ref.at[pl.ds(0, n_tok), :], xs_ref.at[pl.ds(0, n_tok), :], sem).wait()

    @pl.when(i > 0)
    def _():
        wait_step()

    @pl.when(i == pl.num_programs(0) - 1)
    def _():
        wait_step()


def _dispatch(pad_start, rt3, hf, p_rows):
    t, d = hf.shape
    td = MOE_TOK
    xs0 = jnp.zeros((p_rows, d), F32)
    grid_spec = pltpu.PrefetchScalarGridSpec(
        num_scalar_prefetch=1,
        grid=(t // td,),
        in_specs=[pl.BlockSpec((1, 1, 2 * td), lambda i, ps: (i, 0, 0), memory_space=pltpu.SMEM),
                  pl.BlockSpec(memory_space=pl.ANY),
                  pl.BlockSpec(memory_space=pl.ANY)],
        out_specs=pl.BlockSpec(memory_space=pl.ANY),
        scratch_shapes=[pltpu.SemaphoreType.DMA(())],
    )
    return pl.pallas_call(
        _dispatch_kernel,
        out_shape=jax.ShapeDtypeStruct((p_rows, d), F32),
        grid_spec=grid_spec,
        input_output_aliases={3: 0},
        compiler_params=_cparams(("arbitrary",)),
        name="moe_dispatch",
    )(pad_start, rt3, hf, xs0)


def _expert_mlp(xb, wg_b, wu_b, wd_b):
    hid = _silu(jnp.dot(xb, wg_b[...], preferred_element_type=F32)) * jnp.dot(xb, wu_b[...], preferred_element_type=F32)
    return jnp.dot(hid.astype(BF16), wd_b[...], preferred_element_type=F32)


def _expert_kernel(lo_ref, hi_ref, nu_ref, xs_ref, wg0_ref, wu0_ref, wd0_ref, wg1_ref, wu1_ref, wd1_ref, ys_ref,
                   wg0_b, wu0_b, wd0_b, wg1_b, wu1_b, wd1_b):
    i = pl.program_id(0)
    used = i < nu_ref[0]
    prev = jnp.maximum(i - 1, 0)

    @pl.when(used & ((i == 0) | (lo_ref[i] != lo_ref[prev])))
    def _():
        wg0_b[...] = wg0_ref[...].astype(BF16)
        wu0_b[...] = wu0_ref[...].astype(BF16)
        wd0_b[...] = wd0_ref[...].astype(BF16)

    @pl.when(used & ((i == 0) | (hi_ref[i] != hi_ref[prev])))
    def _():
        wg1_b[...] = wg1_ref[...].astype(BF16)
        wu1_b[...] = wu1_ref[...].astype(BF16)
        wd1_b[...] = wd1_ref[...].astype(BF16)

    @pl.when(used)
    def _():
        d = ys_ref.shape[1]
        xb = xs_ref[:, 0:d].astype(BF16)
        gates = xs_ref[:, d:d + LANES]
        ys_ref[...] = (gates[:, 0:1] * _expert_mlp(xb, wg0_b, wu0_b, wd0_b)
                       + gates[:, 1:2] * _expert_mlp(xb, wg1_b, wu1_b, wd1_b))

    @pl.when(jnp.logical_not(used))
    def _():
        ys_ref[...] = jnp.zeros(ys_ref.shape, F32)


def _experts(blk_lo, blk_hi, n_used, xs, w_gate, w_up, w_down, layer):
    p_rows, dx = xs.shape
    d = dx - LANES
    bm = MOE_ROWS
    wspec = lambda shape, which: pl.BlockSpec(
        (None, None) + shape, lambda i, lo, hi, nu: (layer, (lo, hi)[which][i], 0, 0))
    grid_spec = pltpu.PrefetchScalarGridSpec(
        num_scalar_prefetch=3,
        grid=(p_rows // bm,),
        in_specs=[pl.BlockSpec((bm, dx), lambda i, lo, hi, nu: (i, 0)),
                  wspec((d, D_EXPERT), 0), wspec((d, D_EXPERT), 0), wspec((D_EXPERT, d), 0),
                  wspec((d, D_EXPERT), 1), wspec((d, D_EXPERT), 1), wspec((D_EXPERT, d), 1)],
        out_specs=pl.BlockSpec((bm, d), lambda i, lo, hi, nu: (i, 0)),
        scratch_shapes=[pltpu.VMEM((d, D_EXPERT), BF16), pltpu.VMEM((d, D_EXPERT), BF16),
                        pltpu.VMEM((D_EXPERT, d), BF16)] * 2,
    )
    return pl.pallas_call(
        _expert_kernel,
        out_shape=jax.ShapeDtypeStruct((p_rows, d), F32),
        grid_spec=grid_spec,
        compiler_params=_cparams(("arbitrary",)),
        name="moe_experts",
    )(blk_lo, blk_hi, n_used, xs, w_gate, w_up, w_down, w_gate, w_up, w_down)


def _combine_kernel(ps_ref, rt_ref, rt_next_ref, x_ref, gt_ref, fg_ref, ys_ref, o_ref, ybuf, sems, *, final):
    i = pl.program_id(0)
    n_tok = x_ref.shape[0]
    slot = lax.rem(i, 2)

    def gather(rt, dst_slot):
        def issue(g, carry):
            for uu in range(DMA_UNROLL):
                r = g * DMA_UNROLL + uu
                _row_copy(ys_ref, _dest_row(ps_ref, rt, r), ybuf.at[dst_slot], r, sems.at[dst_slot]).start()
            return carry

        lax.fori_loop(0, n_tok // DMA_UNROLL, issue, 0)

    @pl.when(i == 0)
    def _():
        gather(rt_ref, 0)

    @pl.when(i + 1 < pl.num_programs(0))
    def _():
        gather(rt_next_ref, 1 - slot)

    pltpu.make_async_copy(ys_ref.at[pl.ds(0, n_tok), :], ybuf.at[slot], sems.at[slot]).wait()
    x = x_ref[...] + gt_ref[0] * ybuf[slot]
    if final:
        x = x * lax.rsqrt(jnp.mean(x * x, -1, keepdims=True) + NORM_EPS) * fg_ref[...]
    o_ref[...] = x


def _combine(pad_start, rt3, x, gt, final_g, ys, seq, final):
    t, d = x.shape
    tc = MOE_TOK
    per_b = seq // tc
    n_steps = t // tc
    grid_spec = pltpu.PrefetchScalarGridSpec(
        num_scalar_prefetch=1,
        grid=(n_steps,),
        in_specs=[pl.BlockSpec((1, 1, 2 * tc), lambda i, ps: (i, 0, 0), memory_space=pltpu.SMEM),
                  pl.BlockSpec((1, 1, 2 * tc), lambda i, ps: (jnp.minimum(i + 1, n_steps - 1), 0, 0),
                               memory_space=pltpu.SMEM),
                  pl.BlockSpec((tc, d), lambda i, ps: (i, 0)),
                  pl.BlockSpec((1, 1, d), lambda i, ps: (i // per_b, 0, 0)),
                  pl.BlockSpec((1, d), lambda i, ps: (0, 0)),
                  pl.BlockSpec(memory_space=pl.ANY)],
        out_specs=pl.BlockSpec((tc, d), lambda i, ps: (i, 0)),
        scratch_shapes=[pltpu.VMEM((2, tc, d), F32), pltpu.SemaphoreType.DMA((2,))],
    )
    return pl.pallas_call(
        functools.partial(_combine_kernel, final=final),
        out_shape=jax.ShapeDtypeStruct((t, d), F32),
        grid_spec=grid_spec,
        compiler_params=_cparams(("arbitrary",)),
        name="moe_combine",
    )(pad_start, rt3, rt3, x, gt, final_g, ys)


def _bucket_experts():
    lo, hi = [], []
    for g in range(N_GROUPS):
        for a in range(EXPERTS_PER_GROUP):
            for b in range(a + 1, EXPERTS_PER_GROUP):
                lo.append(g * EXPERTS_PER_GROUP + a)
                hi.append(g * EXPERTS_PER_GROUP + b)
    return jnp.asarray(lo, jnp.int32), jnp.asarray(hi, jnp.int32)


def _route_plan(route, counts, t):
    bm = MOE_ROWS
    cnt = counts[0, :N_BUCKETS].astype(jnp.int32)
    padded = (cnt + bm - 1) // bm * bm
    pad_end = jnp.cumsum(padded)
    pad_start = pad_end - padded
    p_rows = t + N_BUCKETS * bm
    n_blk = p_rows // bm
    blk_start = jnp.arange(n_blk, dtype=jnp.int32) * bm
    blk_b = jnp.minimum(jnp.sum((pad_end[None, :] <= blk_start[:, None]).astype(jnp.int32), axis=1), N_BUCKETS - 1)
    lo_tab, hi_tab = _bucket_experts()
    n_used = (pad_end[-1:] // bm).astype(jnp.int32)
    rt3 = route[:, 0:2].astype(jnp.int32).reshape(t // MOE_TOK, 1, 2 * MOE_TOK)
    return pad_start, rt3, lo_tab[blk_b], hi_tab[blk_b], n_used, p_rows


def kernel(x, c, ada_w, ada_b, mix_norm_g, ffn_norm_g, w_in, w_out, gdn_conv_w, gdn_a_log, gdn_dt_bias, gdn_norm_g,
           sgu_ln_g, sgu_ln_b, sgu_w, sgu_b, sc_conv_w, rw_mu, rw_w0, rw_w_up, rw_a0, rw_a_up, rw_g_up, rw_k_k,
           rw_k_a, rw_r_k, rw_gn_g, rw_gn_b, moe_w_group, moe_b_group, moe_w_router, moe_b_router, moe_w_gate,
           moe_w_up, moe_w_down, final_norm_g):
    bn, seq, d = x.shape
    depth = ada_w.shape[0]
    t = bn * seq
    assert d == 4 * GROUP_W and seq % ROW_BLOCK == 0 and ROW_BLOCK % MOE_TOK == 0
    g_w = GROUP_W
    mod = _ada(c, ada_w, ada_b)
    xf = x.reshape(t, d)
    o_z, o_a, o_su = 3 * g_w, 4 * g_w, 4 * g_w + 2 * N_HEADS
    o_rp = o_su + 5 * g_w
    for l in range(depth):
        m = mod[l].reshape(bn, 6, 1, d)
        sh_m, sc_m, gt_m, sh_f, sc_f, gt_f = (m[:, i] for i in range(6))
        wl = w_in[l]
        w_ab = wl[:, o_a:o_su]
        w_r = jnp.concatenate([wl[:, 0:o_a], wl[:, o_su:o_rp], wl[:, o_rp:],
                               jnp.pad(w_ab, ((0, 0), (0, LANES - 2 * N_HEADS)))], axis=1).astype(BF16)
        pg, psc, prw, pab, abt = _in_proj(xf, mix_norm_g[l][None, :], sh_m, sc_m, w_r, w_ab.T.astype(BF16), seq)
        oa = _gdn(pg, pab, abt, gdn_conv_w[l], gdn_a_log[l], gdn_dt_bias[l], gdn_norm_g[l], bn, seq)
        obc = _sgu_conv(psc, sgu_ln_g[l], sgu_ln_b[l], sgu_w[l], sgu_b[l], sc_conv_w[l], bn, seq)
        od = _rwkv(prw, rw_mu[l], rw_w0[l], rw_w_up[l], rw_a0[l], rw_a_up[l], rw_g_up[l], rw_k_k[l], rw_k_a[l],
                   rw_r_k[l], rw_gn_g[l], rw_gn_b[l], bn, seq)
        w_rt = jnp.concatenate([moe_w_router[l], moe_w_group[l],
                                jnp.zeros((d, LANES - N_EXPERTS - N_GROUPS), F32)], axis=1)
        wr_hi = w_rt.astype(BF16)
        wr_lo = (w_rt - wr_hi.astype(F32)).astype(BF16)
        b_r = jnp.concatenate([moe_b_router[l], moe_b_group[l], jnp.zeros((LANES - N_EXPERTS - N_GROUPS,), F32)])[None, :]
        xf, hf, route, counts = _out_router(xf, oa, obc, od, w_out[l].astype(BF16), gt_m, ffn_norm_g[l][None, :],
                                            sh_f, sc_f, wr_hi, wr_lo, b_r, seq)
        pad_start, rt3, blk_lo, blk_hi, n_used, p_rows = _route_plan(route, counts, t)
        xs = _dispatch(pad_start, rt3, hf, p_rows)
        ys = _experts(blk_lo, blk_hi, n_used, xs, moe_w_gate, moe_w_up, moe_w_down, l)
        xf = _combine(pad_start, rt3, xf, gt_f, final_norm_g[None, :], ys, seq, final=(l == depth - 1))
    return xf.reshape(bn, seq, d)
```

```python
import functools
import math

import jax
import jax.numpy as jnp
from jax import lax
from jax.experimental import pallas as pl
from jax.experimental.pallas import tpu as pltpu

F32 = jnp.float32
BF16 = jnp.bfloat16

HEAD_DIM = 64
N_HEADS = 4
GROUP_W = HEAD_DIM * N_HEADS
CHUNK = 64
TILE = 2 * CHUNK
NORM_EPS = 1e-6
SGU_CHUNK = 128
SGU_LN_EPS = 1e-5
RW_GN_EPS = 64e-5
RW_LORA_W, RW_LORA_A, RW_LORA_G = 64, 64, 128
N_GROUPS = 8
EXPERTS_PER_GROUP = 8
N_EXPERTS = N_GROUPS * EXPERTS_PER_GROUP
D_EXPERT = 256
PAIRS_PER_GROUP = EXPERTS_PER_GROUP * (EXPERTS_PER_GROUP - 1) // 2
N_BUCKETS = N_GROUPS * PAIRS_PER_GROUP
LANES = 128
VMEM_LIMIT = 56 * 1024 * 1024

ROW_BLOCK = 512
MOE_ROWS = 128
MOE_TOK = 512


def _dot(a, b):
    return jnp.dot(a.astype(BF16), b.astype(BF16), preferred_element_type=F32)


def _dot_nt(a, b):
    return lax.dot_general(a.astype(BF16), b.astype(BF16), (((1,), (1,)), ((), ())),
                           preferred_element_type=F32)


def _dot_tn(a, b):
    return lax.dot_general(a.astype(BF16), b.astype(BF16), (((0,), (0,)), ((), ())),
                           preferred_element_type=F32)


def _split2(x):
    hi = x.astype(BF16)
    lo = (x - hi.astype(F32)).astype(BF16)
    return hi, lo


def _dot_x_exact(x, m):
    hi, lo = _split2(x)
    return (jnp.dot(hi, m, preferred_element_type=F32) + jnp.dot(lo, m, preferred_element_type=F32))


def _dot_exact_x(m, x):
    hi, lo = _split2(x)
    return (jnp.dot(m, hi, preferred_element_type=F32) + jnp.dot(m, lo, preferred_element_type=F32))


def _sigmoid(x):
    return 1.0 / (1.0 + jnp.exp(-x))


def _silu(x):
    return x * _sigmoid(x)


def _softplus(x):
    return jnp.maximum(x, 0.0) + jnp.log(1.0 + jnp.exp(-jnp.abs(x)))


def _head_masks():
    lane = lax.broadcasted_iota(jnp.int32, (1, GROUP_W), 1)
    return [((lane >> 6) == h).astype(F32) for h in range(N_HEADS)]


def _tile_masks():
    ri = lax.broadcasted_iota(jnp.int32, (TILE, TILE), 0)
    ci = lax.broadcasted_iota(jnp.int32, (TILE, TILE), 1)
    same = (ri >> 6) == (ci >> 6)
    return same & (ri > ci), same & (ri >= ci), (ri == ci).astype(F32)


def _block_diag_mask():
    ri = lax.broadcasted_iota(jnp.int32, (GROUP_W, GROUP_W), 0)
    ci = lax.broadcasted_iota(jnp.int32, (GROUP_W, GROUP_W), 1)
    return ((ri >> 6) == (ci >> 6)).astype(F32)


def _merge_masks():
    ri = lax.broadcasted_iota(jnp.int32, (TILE, TILE), 0)
    ci = lax.broadcasted_iota(jnp.int32, (TILE, TILE), 1)
    return [((ri >> (l + 1)) == (ci >> (l + 1))) & (((ri >> l) & 1) == 1) & (((ci >> l) & 1) == 0) for l in range(6)]


def _unit_lower_inverses(a, eye, chains):
    masks = _merge_masks()
    d = {c: eye - jnp.where(masks[0], a[c], 0.0) for c in chains}
    for l in range(1, 6):
        f = {c: _dot(jnp.where(masks[l], a[c], 0.0), d[c]) for c in chains}
        d = {c: d[c] - _dot(d[c], f[c]) for c in chains}
    return d


def _seg_sum(x, seg):
    return jnp.dot(x.astype(BF16), seg, preferred_element_type=F32)


def _cparams(sem):
    return pltpu.CompilerParams(dimension_semantics=sem, vmem_limit_bytes=VMEM_LIMIT)


def _ada_kernel(c_ref, w_ref, b_ref, o_ref):
    c = c_ref[...]
    ca = _silu(c)
    chi, clo = _split2(ca)
    w = w_ref[...]
    whi, wlo = _split2(w)
    acc = jnp.dot(chi, whi, preferred_element_type=F32)
    acc += jnp.dot(clo, whi, preferred_element_type=F32)
    acc += jnp.dot(chi, wlo, preferred_element_type=F32)
    o_ref[...] = acc + b_ref[...]


def _ada(c, ada_w, ada_b):
    depth, d, d6 = ada_w.shape
    bn = c.shape[0]
    nj = d6 // d
    return pl.pallas_call(
        _ada_kernel,
        out_shape=jax.ShapeDtypeStruct((depth, bn, d6), F32),
        grid=(depth, nj),
        in_specs=[pl.BlockSpec((bn, d), lambda l, j: (0, 0)),
                  pl.BlockSpec((None, d, d), lambda l, j: (l, 0, j)),
                  pl.BlockSpec((None, 1, d), lambda l, j: (l, 0, j))],
        out_specs=pl.BlockSpec((None, bn, d), lambda l, j: (l, 0, j)),
        compiler_params=_cparams(("arbitrary", "arbitrary")),
        name="ada_mod",
    )(c, ada_w, ada_b.reshape(depth, 1, d6))


def _in_proj_kernel(x_ref, g_ref, sh_ref, sc_ref, w_ref, wab_ref, pg_ref, psc_ref, prw_ref, pab_ref, abt_ref):
    x = x_ref[...]
    y = x * lax.rsqrt(jnp.mean(x * x, -1, keepdims=True) + NORM_EPS) * g_ref[...]
    h = (y * (1.0 + sc_ref[0]) + sh_ref[0]).astype(BF16)
    o = 0
    for ref in (pg_ref, psc_ref, prw_ref, pab_ref):
        w = ref.shape[1]
        ref[...] = jnp.dot(h, w_ref[:, o:o + w], preferred_element_type=F32)
        o += w
    abt_ref[...] = lax.dot_general(wab_ref[...], h, (((1,), (1,)), ((), ())), preferred_element_type=F32)


def _in_proj(x, g, shift, scale, w_r, w_abt, seq):
    t, d = x.shape
    tm = ROW_BLOCK
    per_b = seq // tm
    widths = (4 * GROUP_W, 5 * GROUP_W, 4 * GROUP_W, LANES)
    bspec = pl.BlockSpec((1, 1, d), lambda i: (i // per_b, 0, 0))
    return pl.pallas_call(
        _in_proj_kernel,
        out_shape=tuple(jax.ShapeDtypeStruct((t, w), F32) for w in widths) + (jax.ShapeDtypeStruct((8, t), F32),),
        grid=(t // tm,),
        in_specs=[pl.BlockSpec((tm, d), lambda i: (i, 0)),
                  pl.BlockSpec((1, d), lambda i: (0, 0)),
                  bspec, bspec,
                  pl.BlockSpec(w_r.shape, lambda i: (0, 0)),
                  pl.BlockSpec(w_abt.shape, lambda i: (0, 0))],
        out_specs=tuple(pl.BlockSpec((tm, w), lambda i: (i, 0)) for w in widths) + (pl.BlockSpec((8, tm), lambda i: (0, i)),),
        compiler_params=_cparams(("arbitrary",)),
        name="in_proj",
    )(x, g, shift, scale, w_r, w_abt)


def _gdn_kernel(p_ref, ab_ref, abt_ref, cw_ref, alog_ref, dtb_ref, alogt_ref, dtbt_ref, ng_ref,
                eg_ref, eb_ref, seg_ref, tri_ref, trit_ref, full_ref,
                o_ref,
                xbuf, s_ref, q_s, k_s, kb_s, rhs_s, qd_s, kt_s, cd_s, gc4_s, gct_s, u_s, w_s, snap_s):
    rows = p_ref.shape[0]
    j = pl.program_id(1)

    @pl.when(j == 0)
    def _():
        xbuf[0:8, :] = jnp.zeros((8, xbuf.shape[1]), F32)
        s_ref[...] = jnp.zeros(s_ref.shape, F32)

    xbuf[8:8 + rows, :] = p_ref[:, 0:3 * GROUP_W]
    acc = cw_ref[3:4, :] * xbuf[8:8 + rows, :]
    for tap in range(3):
        acc = acc + cw_ref[tap:tap + 1, :] * xbuf[5 + tap:5 + tap + rows, :]
    xbuf[0:8, :] = xbuf[rows:rows + 8, :]
    qkv = _silu(acc)
    seg = seg_ref[...]
    q = qkv[:, 0:GROUP_W]
    k = qkv[:, GROUP_W:2 * GROUP_W]
    v = qkv[:, 2 * GROUP_W:3 * GROUP_W]
    q = q * lax.rsqrt(_seg_sum(q * q, seg) + 1e-6) * (HEAD_DIM ** -0.5)
    k = k * lax.rsqrt(_seg_sum(k * k, seg) + 1e-6)

    ab = ab_ref[...]
    g = -jnp.exp(alog_ref[...]) * _softplus(ab + dtb_ref[...])
    beta = jnp.dot(_sigmoid(ab).astype(BF16), eb_ref[...], preferred_element_type=F32)
    tsl = [slice(t * TILE, (t + 1) * TILE) for t in range(rows // TILE)]
    gc4 = jnp.concatenate([_dot_exact_x(tri_ref[...], g[sl]) for sl in tsl], axis=0)
    gl4 = jnp.concatenate([_dot_exact_x(full_ref[...], g[sl]) for sl in tsl], axis=0)
    gc4_s[...] = gc4
    gc = _dot_x_exact(gc4, eg_ref[...])
    gl = _dot_x_exact(gl4, eg_ref[...])
    egc = jnp.exp(gc)
    kb = k * beta
    q_s[...] = q
    k_s[...] = k
    kb_s[...] = kb
    rhs_s[:, 0:GROUP_W] = v * beta
    rhs_s[:, GROUP_W:2 * GROUP_W] = kb * egc
    qd_s[...] = q * egc
    kt_s[...] = k * jnp.exp(gl - gc)
    cd_s[...] = jnp.exp(gl)
    abt = abt_ref[...]
    gt = -jnp.exp(alogt_ref[...]) * _softplus(abt + dtbt_ref[...])
    for t in range(rows // TILE):
        gct_s[t] = _dot_x_exact(gt[:, tsl[t]], trit_ref[...])

    hm = _head_masks()
    strict, incl, eye = _tile_masks()
    bd = _block_diag_mask()
    ng = ng_ref[...]
    tiles = range(rows // TILE)
    heads = range(N_HEADS)
    chains = [(t, h) for t in tiles for h in heads]
    rsl = [slice(t * TILE, (t + 1) * TILE) for t in tiles]

    x, attn = {}, {}
    for t in tiles:
        kt_ = k_s[rsl[t], :].astype(BF16)
        kbt, qt = kb_s[rsl[t], :], q_s[rsl[t], :]
        gct_t = gct_s[t]
        gc4_t = gc4_s[rsl[t], :]
        for h in heads:
            dec = jnp.exp(jnp.where(incl, gc4_t[:, h:h + 1] - gct_t[h:h + 1, :], -jnp.inf))
            x[t, h] = jnp.where(strict, _dot_nt(kbt * hm[h], kt_) * dec, 0.0)
            attn[t, h] = (_dot_nt(qt * hm[h], kt_) * dec).astype(BF16)
    p = _unit_lower_inverses(x, eye, chains)
    for t in tiles:
        rhs = rhs_s[rsl[t], :].astype(BF16)
        u = jnp.zeros((TILE, GROUP_W), F32)
        w = jnp.zeros((TILE, GROUP_W), F32)
        for h in heads:
            sol = _dot(p[t, h], rhs)
            u = u + hm[h] * sol[:, 0:GROUP_W]
            w = w + hm[h] * sol[:, GROUP_W:2 * GROUP_W]
        u_s[rsl[t], :] = u
        w_s[rsl[t], :] = w

    chunks = range(rows // CHUNK)
    csl = [slice(c * CHUNK, (c + 1) * CHUNK) for c in chunks]
    pq = [_dot_tn(kt_s[csl[c], :], jnp.concatenate([w_s[csl[c], :], u_s[csl[c], :]], axis=1)) for c in chunks]
    pmat = [(bd * pq[c][:, 0:GROUP_W]).astype(BF16) for c in chunks]
    s = s_ref[...]
    for c in chunks:
        snap_s[c] = s.astype(BF16)
        s = s * cd_s[c * CHUNK:c * CHUNK + 1, :] - _dot(pmat[c], s) + bd * pq[c][:, GROUP_W:2 * GROUP_W]
    s_ref[...] = s
    for c in chunks:
        ws = jnp.dot(jnp.concatenate([w_s[csl[c], :], qd_s[csl[c], :]], axis=0).astype(BF16), snap_s[c],
                     preferred_element_type=F32)
        u_s[csl[c], :] = u_s[csl[c], :] - ws[0:CHUNK]
        w_s[csl[c], :] = ws[CHUNK:2 * CHUNK]

    for t in tiles:
        vn = u_s[rsl[t], :].astype(BF16)
        o = w_s[rsl[t], :]
        for h in heads:
            o = o + hm[h] * _dot(attn[t, h], vn)
        o = o * lax.rsqrt(_seg_sum(o * o, seg) * (1.0 / HEAD_DIM) + NORM_EPS) * ng
        o_ref[rsl[t], :] = (o * _silu(p_ref[rsl[t], 3 * GROUP_W:4 * GROUP_W])).astype(o_ref.dtype)


def _chunk_mats():
    ri = jnp.arange(TILE)[:, None]
    ci = jnp.arange(TILE)[None, :]
    same = (ri // CHUNK) == (ci // CHUNK)
    tri = (same & (ci <= ri)).astype(BF16)
    return tri, tri.T, same.astype(BF16)


def _expand_mats():
    lane = jnp.arange(LANES)[:, None]
    col = jnp.arange(GROUP_W)[None, :]
    eg = (lane == col // HEAD_DIM).astype(BF16)
    eb = (lane == N_HEADS + col // HEAD_DIM).astype(BF16)
    seg = ((jnp.arange(GROUP_W)[:, None] // HEAD_DIM) == (col // HEAD_DIM)).astype(BF16)
    return eg, eb, seg


def _pad_lanes(v, n=LANES):
    return jnp.zeros((1, n), F32).at[0, :v.shape[0]].set(v.astype(F32))


def _gdn(pg, pab, abt, conv_w, a_log, dt_bias, norm_g, bn, seq):
    t = pg.shape[0]
    rows = ROW_BLOCK
    nb = seq // rows
    eg, eb, seg = _expand_mats()
    tri, trit, full = _chunk_mats()
    alog_t = jnp.zeros((8, rows), F32).at[:N_HEADS].set(jnp.broadcast_to(a_log[:, None], (N_HEADS, rows)))
    dtb_t = jnp.zeros((8, rows), F32).at[:N_HEADS].set(jnp.broadcast_to(dt_bias[:, None], (N_HEADS, rows)))
    consts = (conv_w.astype(F32), _pad_lanes(a_log), _pad_lanes(dt_bias), alog_t, dtb_t,
              jnp.tile(norm_g.astype(F32), N_HEADS)[None, :], eg, eb, seg, tri, trit, full)
    rowmap = lambda b, j: (b * nb + j, 0)
    return pl.pallas_call(
        _gdn_kernel,
        out_shape=jax.ShapeDtypeStruct((t, GROUP_W), BF16),
        grid=(bn, nb),
        in_specs=[pl.BlockSpec((rows, 4 * GROUP_W), rowmap),
                  pl.BlockSpec((rows, LANES), rowmap),
                  pl.BlockSpec((8, rows), lambda b, j: (0, b * nb + j))]
                 + [pl.BlockSpec(c.shape, lambda b, j: (0, 0)) for c in consts],
        out_specs=pl.BlockSpec((rows, GROUP_W), rowmap),
        scratch_shapes=[pltpu.VMEM((rows + 8, 3 * GROUP_W), F32),
                        pltpu.VMEM((GROUP_W, GROUP_W), F32),
                        pltpu.VMEM((rows, GROUP_W), F32), pltpu.VMEM((rows, GROUP_W), F32),
                        pltpu.VMEM((rows, GROUP_W), F32), pltpu.VMEM((rows, 2 * GROUP_W), F32),
                        pltpu.VMEM((rows, GROUP_W), F32), pltpu.VMEM((rows, GROUP_W), F32),
                        pltpu.VMEM((rows, GROUP_W), F32), pltpu.VMEM((rows, LANES), F32),
                        pltpu.VMEM((rows // TILE, 8, TILE), F32),
                        pltpu.VMEM((rows, GROUP_W), F32), pltpu.VMEM((rows, GROUP_W), F32),
                        pltpu.VMEM((rows // CHUNK, GROUP_W, GROUP_W), BF16)],
        compiler_params=_cparams(("arbitrary", "arbitrary")),
        name="gdn_mixer",
    )(pg, pab, abt, *consts)


def _sgu_conv_kernel(p_ref, lng_ref, lnb_ref, ws_ref, bs_ref, cw_ref, o_ref, xbuf):
    rows = p_ref.shape[0]
    j = pl.program_id(1)

    @pl.when(j == 0)
    def _():
        xbuf[0:8, :] = jnp.zeros((8, GROUP_W), F32)

    u = jax.nn.gelu(p_ref[:, 0:GROUP_W])
    vf = jax.nn.gelu(p_ref[:, GROUP_W:2 * GROUP_W])
    mean = jnp.mean(vf, -1, keepdims=True)
    var = jnp.mean(jnp.square(vf - mean), -1, keepdims=True)
    v = (vf - mean) * lax.rsqrt(var + SGU_LN_EPS) * lng_ref[...] + lnb_ref[...]
    hm = _head_masks()
    ri = lax.broadcasted_iota(jnp.int32, (SGU_CHUNK, SGU_CHUNK), 0)
    ci = lax.broadcasted_iota(jnp.int32, (SGU_CHUNK, SGU_CHUNK), 1)
    ws = [jnp.where(ri >= ci, ws_ref[h], 0.0).astype(BF16) for h in range(N_HEADS)]
    bs = bs_ref[...]
    for c in range(rows // SGU_CHUNK):
        cs = slice(c * SGU_CHUNK, (c + 1) * SGU_CHUNK)
        vc = v[cs].astype(BF16)
        mixed = bs
        for h in range(N_HEADS):
            mixed = mixed + hm[h] * jnp.dot(ws[h], vc, preferred_element_type=F32)
        o_ref[cs, 0:GROUP_W] = (u[cs] * mixed).astype(o_ref.dtype)

    xbuf[8:8 + rows, :] = p_ref[:, 3 * GROUP_W:4 * GROUP_W] * p_ref[:, 4 * GROUP_W:5 * GROUP_W]
    acc = cw_ref[2:3, :] * xbuf[8:8 + rows, :]
    for tap in range(2):
        acc = acc + cw_ref[tap:tap + 1, :] * xbuf[6 + tap:6 + tap + rows, :]
    xbuf[0:8, :] = xbuf[rows:rows + 8, :]
    o_ref[:, GROUP_W:2 * GROUP_W] = (p_ref[:, 2 * GROUP_W:3 * GROUP_W] * acc).astype(o_ref.dtype)


def _sgu_conv(psc, ln_g, ln_b, w_s, b_s, conv_w, bn, seq):
    t = psc.shape[0]
    rows = ROW_BLOCK
    nb = seq // rows
    bs_exp = jnp.repeat(b_s.T.astype(F32), HEAD_DIM, axis=1)
    consts = (ln_g[None, :].astype(F32), ln_b[None, :].astype(F32), w_s.astype(F32), bs_exp, conv_w.astype(F32))
    rowmap = lambda b, j: (b * nb + j, 0)
    return pl.pallas_call(
        _sgu_conv_kernel,
        out_shape=jax.ShapeDtypeStruct((t, 2 * GROUP_W), BF16),
        grid=(bn, nb),
        in_specs=[pl.BlockSpec((rows, 5 * GROUP_W), rowmap)]
                 + [pl.BlockSpec(c.shape, lambda b, j, n=c.ndim: (0,) * n) for c in consts],
        out_specs=pl.BlockSpec((rows, 2 * GROUP_W), rowmap),
        scratch_shapes=[pltpu.VMEM((rows + 8, GROUP_W), F32)],
        compiler_params=_cparams(("arbitrary", "arbitrary")),
        name="sgu_conv_mixer",
    )(psc, *consts)


def _rwkv_kernel(p_ref, mu_ref, w0_ref, wup_ref, a0_ref, aup_ref, gup_ref, kk_ref, ka_ref, rk_ref, gng_ref, gnb_ref,
                 seg_ref, tri_ref, full_ref,
                 o_ref,
                 prev, s_ref, at_s, bt_s, kt_s, rt_s, v_s, btl_s, ktl_s, gam_s, bon_s, gate_s, wa_s, u_s, y_s, snap_s):
    rows = p_ref.shape[0]
    j = pl.program_id(1)

    @pl.when(j == 0)
    def _():
        prev[...] = jnp.zeros(prev.shape, F32)
        s_ref[...] = jnp.zeros(s_ref.shape, F32)

    prev[8:8 + rows, :] = p_ref[...]
    p = p_ref[...]
    p = p + (prev[7:7 + rows, :] - p) * mu_ref[...]
    prev[0:8, :] = prev[rows:rows + 8, :]
    g_w = GROUP_W
    r = p[:, 0:g_w]
    k = p[:, g_w:2 * g_w]
    v = p[:, 2 * g_w:3 * g_w]
    o = 3 * g_w
    xw = p[:, o:o + RW_LORA_W]
    xa = p[:, o + RW_LORA_W:o + RW_LORA_W + RW_LORA_A]
    xg = p[:, o + RW_LORA_W + RW_LORA_A:o + RW_LORA_W + RW_LORA_A + RW_LORA_G]
    w_log = -_softplus(-(w0_ref[...] + _dot(jnp.tanh(xw), wup_ref[...]))) - 0.5
    lw = -jnp.exp(w_log)
    a = _sigmoid(a0_ref[...] + _dot(xa, aup_ref[...]))
    gate_s[...] = _dot(_sigmoid(xg), gup_ref[...])
    seg = seg_ref[...]
    kk = k * kk_ref[...]
    kk = kk * lax.rsqrt(_seg_sum(kk * kk, seg) + 1e-12)
    k_mod = k * (1.0 + (a - 1.0) * ka_ref[...])
    bon_s[...] = _seg_sum(r * k_mod * rk_ref[...], seg) * v
    tsl = [slice(t * TILE, (t + 1) * TILE) for t in range(rows // TILE)]
    cl = jnp.concatenate([_dot_exact_x(tri_ref[...], lw[sl]) for sl in tsl], axis=0)
    ct = jnp.concatenate([_dot_exact_x(full_ref[...], lw[sl]) for sl in tsl], axis=0)
    e_neg = jnp.exp(-cl)
    e_tail = jnp.exp(ct - cl)
    zb = kk * a
    at_s[...] = -kk * jnp.exp(cl - lw)
    bt_s[...] = zb * e_neg
    kt_s[...] = k_mod * e_neg
    rt_s[...] = r * jnp.exp(cl)
    v_s[...] = v
    btl_s[...] = zb * e_tail
    ktl_s[...] = k_mod * e_tail
    gam_s[...] = jnp.exp(ct)

    hm = _head_masks()
    strict, incl, eye = _tile_masks()
    bd = _block_diag_mask()
    gng, gnb = gng_ref[...], gnb_ref[...]

    tiles = range(rows // TILE)
    heads = range(N_HEADS)
    chains = [(t, h) for t in tiles for h in heads]
    rsl = [slice(t * TILE, (t + 1) * TILE) for t in tiles]

    x, lak, mrb, mrk = {}, {}, {}, {}
    for t in tiles:
        at, rt = at_s[rsl[t], :], rt_s[rsl[t], :]
        rhs_nt = jnp.concatenate([bt_s[rsl[t], :], kt_s[rsl[t], :]], axis=0).astype(BF16)
        for h in heads:
            sc = _dot_nt(jnp.concatenate([at * hm[h], rt * hm[h]], axis=0), rhs_nt)
            x[t, h] = jnp.where(strict, -sc[0:TILE, 0:TILE], 0.0)
            lak[t, h] = jnp.where(strict, sc[0:TILE, TILE:2 * TILE], 0.0).astype(BF16)
            mrb[t, h] = jnp.where(incl, sc[TILE:2 * TILE, 0:TILE], 0.0).astype(BF16)
            mrk[t, h] = jnp.where(incl, sc[TILE:2 * TILE, TILE:2 * TILE], 0.0).astype(BF16)
    p = _unit_lower_inverses(x, eye, chains)
    for t in tiles:
        at = at_s[rsl[t], :]
        vt = v_s[rsl[t], :].astype(BF16)
        wa = jnp.zeros((TILE, GROUP_W), F32)
        u0 = jnp.zeros((TILE, GROUP_W), F32)
        y0 = jnp.zeros((TILE, GROUP_W), F32)
        for h in heads:
            sol = _dot(p[t, h], jnp.concatenate([at, _dot(lak[t, h], vt)], axis=1))
            wa = wa + hm[h] * sol[:, 0:GROUP_W]
            u0 = u0 + hm[h] * sol[:, GROUP_W:2 * GROUP_W]
            y0 = y0 + hm[h] * _dot(mrk[t, h], vt)
        wa_s[rsl[t], :] = wa
        u_s[rsl[t], :] = u0
        y_s[rsl[t], :] = y0

    chunks = range(rows // CHUNK)
    csl = [slice(c * CHUNK, (c + 1) * CHUNK) for c in chunks]
    pmat = [(bd * _dot_tn(wa_s[csl[c], :], btl_s[csl[c], :])).astype(BF16) for c in chunks]
    qmat = [bd * _dot_tn(jnp.concatenate([u_s[csl[c], :], v_s[csl[c], :]], axis=0),
                         jnp.concatenate([btl_s[csl[c], :], ktl_s[csl[c], :]], axis=0)) for c in chunks]
    s = s_ref[...]
    for c in chunks:
        snap_s[c] = s.astype(BF16)
        s = s * gam_s[c * CHUNK:c * CHUNK + 1, :] + _dot(s, pmat[c]) + qmat[c]
    s_ref[...] = s
    for c in chunks:
        us = _dot_nt(jnp.concatenate([wa_s[csl[c], :], rt_s[csl[c], :]], axis=0), snap_s[c])
        u_s[csl[c], :] = u_s[csl[c], :] + us[0:CHUNK]
        y_s[csl[c], :] = y_s[csl[c], :] + us[CHUNK:2 * CHUNK]

    for t in tiles:
        u = u_s[rsl[t], :].astype(BF16)
        y = y_s[rsl[t], :]
        for h in heads:
            y = y + hm[h] * _dot(mrb[t, h], u)
        mean = _seg_sum(y, seg) * (1.0 / HEAD_DIM)
        yc = y - mean
        var = _seg_sum(yc * yc, seg) * (1.0 / HEAD_DIM)
        yn = yc * lax.rsqrt(var + RW_GN_EPS) * gng + gnb
        o_ref[rsl[t], :] = ((yn + bon_s[rsl[t], :]) * gate_s[rsl[t], :]).astype(o_ref.dtype)


def _rwkv(prw, mu, w0, w_up, a0, a_up, g_up, k_k, k_a, r_k, gn_g, gn_b, bn, seq):
    t = prw.shape[0]
    rows = ROW_BLOCK
    nb = seq // rows
    _, _, seg = _expand_mats()
    tri, _, full = _chunk_mats()
    row = lambda x: x.reshape(1, -1).astype(F32)
    consts = (row(mu), row(w0), w_up.astype(BF16), row(a0), a_up.astype(BF16), g_up.astype(BF16),
              row(k_k), row(k_a), row(r_k), row(gn_g), row(gn_b), seg, tri, full)
    rowmap = lambda b, j: (b * nb + j, 0)
    big = lambda: pltpu.VMEM((rows, GROUP_W), F32)
    return pl.pallas_call(
        _rwkv_kernel,
        out_shape=jax.ShapeDtypeStruct((t, GROUP_W), BF16),
        grid=(bn, nb),
        in_specs=[pl.BlockSpec((rows, 4 * GROUP_W), rowmap)]
                 + [pl.BlockSpec(c.shape, lambda b, j: (0, 0)) for c in consts],
        out_specs=pl.BlockSpec((rows, GROUP_W), rowmap),
        scratch_shapes=[pltpu.VMEM((rows + 8, 4 * GROUP_W), F32), pltpu.VMEM((GROUP_W, GROUP_W), F32)]
                       + [big() for _ in range(13)] + [pltpu.VMEM((rows // CHUNK, GROUP_W, GROUP_W), BF16)],
        compiler_params=_cparams(("arbitrary", "arbitrary")),
        name="rwkv7_mixer",
    )(prw, *consts)


def _out_router_kernel(x_ref, oa_ref, obc_ref, od_ref, wo_ref, gt_ref, g_ref, sh_ref, sc_ref,
                       wrh_ref, wrl_ref, br_ref, tri_ref,
                       xo_ref, hf_ref, route_ref, cnt_ref, carry):
    i = pl.program_id(0)

    @pl.when(i == 0)
    def _():
        carry[...] = jnp.zeros(carry.shape, F32)

    g_w = GROUP_W
    mixed = jnp.dot(oa_ref[...], wo_ref[0:g_w, :], preferred_element_type=F32)
    mixed += jnp.dot(obc_ref[...], wo_ref[g_w:3 * g_w, :], preferred_element_type=F32)
    mixed += jnp.dot(od_ref[...], wo_ref[3 * g_w:4 * g_w, :], preferred_element_type=F32)
    x = x_ref[...] + gt_ref[0] * mixed
    xo_ref[...] = x
    y = x * lax.rsqrt(jnp.mean(x * x, -1, keepdims=True) + NORM_EPS) * g_ref[...]
    hf = y * (1.0 + sc_ref[0]) + sh_ref[0]
    d_model = hf.shape[1]
    hf_ref[:, 0:d_model] = hf
    hh, hl = _split2(hf)
    lg = (jnp.dot(hh, wrh_ref[...], preferred_element_type=F32) + jnp.dot(hl, wrh_ref[...], preferred_element_type=F32)
          + jnp.dot(hh, wrl_ref[...], preferred_element_type=F32) + br_ref[...])
    tm = lg.shape[0]
    lane = lax.broadcasted_iota(jnp.int32, (tm, LANES), 1)
    lanef = lane.astype(F32)
    big = jnp.float32(1e9)
    ninf = jnp.float32(-jnp.inf)
    is_g = (lane >= N_EXPERTS) & (lane < N_EXPERTS + N_GROUPS)
    gl = jnp.where(is_g, lg, ninf)
    gmax = jnp.max(gl, -1, keepdims=True)
    gsel = jnp.min(jnp.where(gl == gmax, lanef - N_EXPERTS, big), -1, keepdims=True)
    p_group = 1.0 / jnp.sum(jnp.where(is_g, jnp.exp(gl - gmax), 0.0), -1, keepdims=True)
    in_grp = (lane < N_EXPERTS) & ((lane >> 3).astype(F32) == gsel)
    el = jnp.where(in_grp, lg, ninf)
    v1 = jnp.max(el, -1, keepdims=True)
    i1 = jnp.min(jnp.where(el == v1, lanef, big), -1, keepdims=True)
    el2 = jnp.where(lanef == i1, ninf, el)
    v2 = jnp.max(el2, -1, keepdims=True)
    i2 = jnp.min(jnp.where(el2 == v2, lanef, big), -1, keepdims=True)
    e21 = jnp.exp(v2 - v1)
    g1 = p_group / (1.0 + e21)
    g2 = p_group * e21 / (1.0 + e21)
    first_lo = i1 < i2
    a = jnp.where(first_lo, i1, i2) - gsel * EXPERTS_PER_GROUP
    b = jnp.where(first_lo, i2, i1) - gsel * EXPERTS_PER_GROUP
    bucket = gsel * PAIRS_PER_GROUP + a * (2 * EXPERTS_PER_GROUP - 1 - a) * 0.5 + (b - a - 1.0)
    g_lo = jnp.where(first_lo, g1, g2)
    g_hi = jnp.where(first_lo, g2, g1)
    hf_ref[:, d_model:d_model + LANES] = jnp.where(lane == 0, g_lo, jnp.where(lane == 1, g_hi, 0.0))
    lane2 = lax.broadcasted_iota(jnp.int32, (tm, 2 * LANES), 1).astype(F32)
    oh = lane2 == bucket
    total = jnp.dot(tri_ref[...], jnp.where(oh, 1.0, 0.0).astype(BF16), preferred_element_type=F32) + carry[...]
    rank = jnp.sum(jnp.where(oh, total, 0.0), -1, keepdims=True)
    carry[...] = carry[...] + jnp.sum(jnp.where(oh, 1.0, 0.0), axis=0, keepdims=True)
    cnt_ref[...] = carry[...]
    route_ref[...] = jnp.where(lane == 0, bucket, jnp.where(lane == 1, rank, 0.0))


def _out_router(x, oa, obc, od, w_out, gt, g, shift, scale, wr_hi, wr_lo, b_r, seq):
    t, d = x.shape
    tm = ROW_BLOCK
    per_b = seq // tm
    tri = (jnp.arange(tm)[:, None] > jnp.arange(tm)[None, :]).astype(BF16)
    bspec = pl.BlockSpec((1, 1, d), lambda i: (i // per_b, 0, 0))
    full = lambda a: pl.BlockSpec(a.shape, lambda i: (0,) * a.ndim)
    return pl.pallas_call(
        _out_router_kernel,
        out_shape=(jax.ShapeDtypeStruct((t, d), F32), jax.ShapeDtypeStruct((t, d + LANES), F32),
                   jax.ShapeDtypeStruct((t, LANES), F32), jax.ShapeDtypeStruct((1, 2 * LANES), F32)),
        grid=(t // tm,),
        in_specs=[pl.BlockSpec((tm, d), lambda i: (i, 0)),
                  pl.BlockSpec((tm, GROUP_W), lambda i: (i, 0)),
                  pl.BlockSpec((tm, 2 * GROUP_W), lambda i: (i, 0)),
                  pl.BlockSpec((tm, GROUP_W), lambda i: (i, 0)),
                  full(w_out), bspec, full(g), bspec, bspec, full(wr_hi), full(wr_lo), full(b_r), full(tri)],
        out_specs=(pl.BlockSpec((tm, d), lambda i: (i, 0)), pl.BlockSpec((tm, d + LANES), lambda i: (i, 0)),
                   pl.BlockSpec((tm, LANES), lambda i: (i, 0)), pl.BlockSpec((1, 2 * LANES), lambda i: (0, 0))),
        scratch_shapes=[pltpu.VMEM((1, 2 * LANES), F32)],
        compiler_params=_cparams(("arbitrary",)),
        name="out_proj_router",
    )(x, oa, obc, od, w_out, gt, g, shift, scale, wr_hi, wr_lo, b_r, tri)


def _row_copy(src, src_row, dst, dst_row, sem):
    return pltpu.make_async_copy(src.at[pl.ds(src_row, 1), :], dst.at[pl.ds(dst_row, 1), :], sem)


DMA_UNROLL = 16


def _dest_row(ps_ref, rt_ref, r):
    return ps_ref[rt_ref[0, 0, 2 * r]] + rt_ref[0, 0, 2 * r + 1]


STAGE_SLOTS = 3


def _dispatch_kernel(ps_ref, rt_ref, hf_ref, xs_in_ref, xs_ref, stage, stage_sems, row_sems):
    del xs_in_ref
    i = pl.program_id(0)
    last = pl.num_programs(0) - 1
    n_tok = stage.shape[1]
    slot = lax.rem(i, STAGE_SLOTS)

    def stage_copy(blk, s):
        return pltpu.make_async_copy(hf_ref.at[pl.ds(blk * n_tok, n_tok), :], stage.at[s], stage_sems.at[s])

    def rows_done(s):
        pltpu.make_async_copy(stage.at[s], xs_ref.at[pl.ds(0, n_tok), :], row_sems.at[s]).wait()

    @pl.when(i == 0)
    def _():
        stage_copy(0, 0).start()

    @pl.when(i < last)
    def _():
        stage_copy(i + 1, lax.rem(i + 1, STAGE_SLOTS)).start()

    stage_copy(i, slot).wait()

    def issue(g, carry):
        for uu in range(DMA_UNROLL):
            r = g * DMA_UNROLL + uu
            _row_copy(stage.at[slot], r, xs_ref, _dest_row(ps_ref, rt_ref, r), row_sems.at[slot]).start()
        return carry

    lax.fori_loop(0, n_tok // DMA_UNROLL, issue, 0)

    @pl.when(i > 0)
    def _():
        rows_done(lax.rem(i + STAGE_SLOTS - 1, STAGE_SLOTS))

    @pl.when(i == last)
    def _():
        rows_done(slot)


def _dispatch(pad_start, rt3, hf, p_rows):
    t, d = hf.shape
    td = MOE_TOK
    xs0 = jnp.zeros((p_rows, d), F32)
    grid_spec = pltpu.PrefetchScalarGridSpec(
        num_scalar_prefetch=1,
        grid=(t // td,),
        in_specs=[pl.BlockSpec((1, 1, 2 * td), lambda i, ps: (i, 0, 0), memory_space=pltpu.SMEM),
                  pl.BlockSpec(memory_space=pl.ANY),
                  pl.BlockSpec(memory_space=pl.ANY)],
        out_specs=pl.BlockSpec(memory_space=pl.ANY),
        scratch_shapes=[pltpu.VMEM((STAGE_SLOTS, td, d), F32), pltpu.SemaphoreType.DMA((STAGE_SLOTS,)),
                        pltpu.SemaphoreType.DMA((STAGE_SLOTS,))],
    )
    return pl.pallas_call(
        _dispatch_kernel,
        out_shape=jax.ShapeDtypeStruct((p_rows, d), F32),
        grid_spec=grid_spec,
        input_output_aliases={3: 0},
        compiler_params=_cparams(("arbitrary",)),
        name="moe_dispatch",
    )(pad_start, rt3, hf, xs0)


def _expert_mlp(xb, wg_b, wu_b, wd_b):
    hid = _silu(jnp.dot(xb, wg_b[...], preferred_element_type=F32)) * jnp.dot(xb, wu_b[...], preferred_element_type=F32)
    return jnp.dot(hid.astype(BF16), wd_b[...], preferred_element_type=F32)


def _expert_kernel(lo_ref, hi_ref, nu_ref, xs_ref, wg0_ref, wu0_ref, wd0_ref, wg1_ref, wu1_ref, wd1_ref, ys_ref,
                   wg0_b, wu0_b, wd0_b, wg1_b, wu1_b, wd1_b):
    i = pl.program_id(0)
    used = i < nu_ref[0]
    prev = jnp.maximum(i - 1, 0)

    @pl.when(used & ((i == 0) | (lo_ref[i] != lo_ref[prev])))
    def _():
        wg0_b[...] = wg0_ref[...].astype(BF16)
        wu0_b[...] = wu0_ref[...].astype(BF16)
        wd0_b[...] = wd0_ref[...].astype(BF16)

    @pl.when(used & ((i == 0) | (hi_ref[i] != hi_ref[prev])))
    def _():
        wg1_b[...] = wg1_ref[...].astype(BF16)
        wu1_b[...] = wu1_ref[...].astype(BF16)
        wd1_b[...] = wd1_ref[...].astype(BF16)

    @pl.when(used)
    def _():
        d = ys_ref.shape[1]
        xb = xs_ref[:, 0:d].astype(BF16)
        gates = xs_ref[:, d:d + LANES]
        ys_ref[...] = (gates[:, 0:1] * _expert_mlp(xb, wg0_b, wu0_b, wd0_b)
                       + gates[:, 1:2] * _expert_mlp(xb, wg1_b, wu1_b, wd1_b))

    @pl.when(jnp.logical_not(used))
    def _():
        ys_ref[...] = jnp.zeros(ys_ref.shape, F32)


def _experts(blk_lo, blk_hi, n_used, xs, w_gate, w_up, w_down, layer):
    p_rows, dx = xs.shape
    d = dx - LANES
    bm = MOE_ROWS
    wspec = lambda shape, which: pl.BlockSpec(
        (None, None) + shape, lambda i, lo, hi, nu: (layer, (lo, hi)[which][i], 0, 0))
    grid_spec = pltpu.PrefetchScalarGridSpec(
        num_scalar_prefetch=3,
        grid=(p_rows // bm,),
        in_specs=[pl.BlockSpec((bm, dx), lambda i, lo, hi, nu: (i, 0)),
                  wspec((d, D_EXPERT), 0), wspec((d, D_EXPERT), 0), wspec((D_EXPERT, d), 0),
                  wspec((d, D_EXPERT), 1), wspec((d, D_EXPERT), 1), wspec((D_EXPERT, d), 1)],
        out_specs=pl.BlockSpec((bm, d), lambda i, lo, hi, nu: (i, 0)),
        scratch_shapes=[pltpu.VMEM((d, D_EXPERT), BF16), pltpu.VMEM((d, D_EXPERT), BF16),
                        pltpu.VMEM((D_EXPERT, d), BF16)] * 2,
    )
    return pl.pallas_call(
        _expert_kernel,
        out_shape=jax.ShapeDtypeStruct((p_rows, d), F32),
        grid_spec=grid_spec,
        compiler_params=_cparams(("arbitrary",)),
        name="moe_experts",
    )(blk_lo, blk_hi, n_used, xs, w_gate, w_up, w_down, w_gate, w_up, w_down)


def _combine_kernel(ps_ref, rt_ref, rt_next_ref, x_ref, gt_ref, fg_ref, ys_ref, o_ref, ybuf, sems, *, final):
    i = pl.program_id(0)
    n_tok = x_ref.shape[0]
    slot = lax.rem(i, 2)

    def gather(rt, dst_slot):
        def issue(g, carry):
            for uu in range(DMA_UNROLL):
                r = g * DMA_UNROLL + uu
                _row_copy(ys_ref, _dest_row(ps_ref, rt, r), ybuf.at[dst_slot], r, sems.at[dst_slot]).start()
            return carry

        lax.fori_loop(0, n_tok // DMA_UNROLL, issue, 0)

    @pl.when(i == 0)
    def _():
        gather(rt_ref, 0)

    @pl.when(i + 1 < pl.num_programs(0))
    def _():
        gather(rt_next_ref, 1 - slot)

    pltpu.make_async_copy(ys_ref.at[pl.ds(0, n_tok), :], ybuf.at[slot], sems.at[slot]).wait()
    x = x_ref[...] + gt_ref[0] * ybuf[slot]
    if final:
        x = x * lax.rsqrt(jnp.mean(x * x, -1, keepdims=True) + NORM_EPS) * fg_ref[...]
    o_ref[...] = x


def _combine(pad_start, rt3, x, gt, final_g, ys, seq, final):
    t, d = x.shape
    tc = MOE_TOK
    per_b = seq // tc
    n_steps = t // tc
    grid_spec = pltpu.PrefetchScalarGridSpec(
        num_scalar_prefetch=1,
        grid=(n_steps,),
        in_specs=[pl.BlockSpec((1, 1, 2 * tc), lambda i, ps: (i, 0, 0), memory_space=pltpu.SMEM),
                  pl.BlockSpec((1, 1, 2 * tc), lambda i, ps: (jnp.minimum(i + 1, n_steps - 1), 0, 0),
                               memory_space=pltpu.SMEM),
                  pl.BlockSpec((tc, d), lambda i, ps: (i, 0)),
                  pl.BlockSpec((1, 1, d), lambda i, ps: (i // per_b, 0, 0)),
                  pl.BlockSpec((1, d), lambda i, ps: (0, 0)),
                  pl.BlockSpec(memory_space=pl.ANY)],
        out_specs=pl.BlockSpec((tc, d), lambda i, ps: (i, 0)),
        scratch_shapes=[pltpu.VMEM((2, tc, d), F32), pltpu.SemaphoreType.DMA((2,))],
    )
    return pl.pallas_call(
        functools.partial(_combine_kernel, final=final),
        out_shape=jax.ShapeDtypeStruct((t, d), F32),
        grid_spec=grid_spec,
        compiler_params=_cparams(("arbitrary",)),
        name="moe_combine",
    )(pad_start, rt3, rt3, x, gt, final_g, ys)


def _bucket_experts():
    lo, hi = [], []
    for g in range(N_GROUPS):
        for a in range(EXPERTS_PER_GROUP):
            for b in range(a + 1, EXPERTS_PER_GROUP):
                lo.append(g * EXPERTS_PER_GROUP + a)
                hi.append(g * EXPERTS_PER_GROUP + b)
    return jnp.asarray(lo, jnp.int32), jnp.asarray(hi, jnp.int32)


def _route_plan(route, counts, t):
    bm = MOE_ROWS
    cnt = counts[0, :N_BUCKETS].astype(jnp.int32)
    padded = (cnt + bm - 1) // bm * bm
    pad_end = jnp.cumsum(padded)
    pad_start = pad_end - padded
    p_rows = t + N_BUCKETS * bm
    n_blk = p_rows // bm
    blk_start = jnp.arange(n_blk, dtype=jnp.int32) * bm
    blk_b = jnp.minimum(jnp.sum((pad_end[None, :] <= blk_start[:, None]).astype(jnp.int32), axis=1), N_BUCKETS - 1)
    lo_tab, hi_tab = _bucket_experts()
    n_used = (pad_end[-1:] // bm).astype(jnp.int32)
    rt3 = route[:, 0:2].astype(jnp.int32).reshape(t // MOE_TOK, 1, 2 * MOE_TOK)
    return pad_start, rt3, lo_tab[blk_b], hi_tab[blk_b], n_used, p_rows


def kernel(x, c, ada_w, ada_b, mix_norm_g, ffn_norm_g, w_in, w_out, gdn_conv_w, gdn_a_log, gdn_dt_bias, gdn_norm_g,
           sgu_ln_g, sgu_ln_b, sgu_w, sgu_b, sc_conv_w, rw_mu, rw_w0, rw_w_up, rw_a0, rw_a_up, rw_g_up, rw_k_k,
           rw_k_a, rw_r_k, rw_gn_g, rw_gn_b, moe_w_group, moe_b_group, moe_w_router, moe_b_router, moe_w_gate,
           moe_w_up, moe_w_down, final_norm_g):
    bn, seq, d = x.shape
    depth = ada_w.shape[0]
    t = bn * seq
    assert d == 4 * GROUP_W and seq % ROW_BLOCK == 0 and ROW_BLOCK % MOE_TOK == 0
    g_w = GROUP_W
    mod = _ada(c, ada_w, ada_b)
    xf = x.reshape(t, d)
    o_z, o_a, o_su = 3 * g_w, 4 * g_w, 4 * g_w + 2 * N_HEADS
    o_rp = o_su + 5 * g_w
    for l in range(depth):
        m = mod[l].reshape(bn, 6, 1, d)
        sh_m, sc_m, gt_m, sh_f, sc_f, gt_f = (m[:, i] for i in range(6))
        wl = w_in[l]
        w_ab = wl[:, o_a:o_su]
        w_r = jnp.concatenate([wl[:, 0:o_a], wl[:, o_su:o_rp], wl[:, o_rp:],
                               jnp.pad(w_ab, ((0, 0), (0, LANES - 2 * N_HEADS)))], axis=1).astype(BF16)
        pg, psc, prw, pab, abt = _in_proj(xf, mix_norm_g[l][None, :], sh_m, sc_m, w_r, w_ab.T.astype(BF16), seq)
        oa = _gdn(pg, pab, abt, gdn_conv_w[l], gdn_a_log[l], gdn_dt_bias[l], gdn_norm_g[l], bn, seq)
        obc = _sgu_conv(psc, sgu_ln_g[l], sgu_ln_b[l], sgu_w[l], sgu_b[l], sc_conv_w[l], bn, seq)
        od = _rwkv(prw, rw_mu[l], rw_w0[l], rw_w_up[l], rw_a0[l], rw_a_up[l], rw_g_up[l], rw_k_k[l], rw_k_a[l],
                   rw_r_k[l], rw_gn_g[l], rw_gn_b[l], bn, seq)
        w_rt = jnp.concatenate([moe_w_router[l], moe_w_group[l],
                                jnp.zeros((d, LANES - N_EXPERTS - N_GROUPS), F32)], axis=1)
        wr_hi = w_rt.astype(BF16)
        wr_lo = (w_rt - wr_hi.astype(F32)).astype(BF16)
        b_r = jnp.concatenate([moe_b_router[l], moe_b_group[l], jnp.zeros((LANES - N_EXPERTS - N_GROUPS,), F32)])[None, :]
        xf, hf, route, counts = _out_router(xf, oa, obc, od, w_out[l].astype(BF16), gt_m, ffn_norm_g[l][None, :],
                                            sh_f, sc_f, wr_hi, wr_lo, b_r, seq)
        pad_start, rt3, blk_lo, blk_hi, n_used, p_rows = _route_plan(route, counts, t)
        xs = _dispatch(pad_start, rt3, hf, p_rows)
        ys = _experts(blk_lo, blk_hi, n_used, xs, moe_w_gate, moe_w_up, moe_w_down, l)
        xf = _combine(pad_start, rt3, xf, gt_f, final_norm_g[None, :], ys, seq, final=(l == depth - 1))
    return xf.reshape(bn, seq, d)
```

```python
import functools
import math

import jax
import jax.numpy as jnp
from jax import lax
from jax.experimental import pallas as pl
from jax.experimental.pallas import tpu as pltpu

F32 = jnp.float32
BF16 = jnp.bfloat16

HEAD_DIM = 64
N_HEADS = 4
GROUP_W = HEAD_DIM * N_HEADS
CHUNK = 64
TILE = 2 * CHUNK
NORM_EPS = 1e-6
SGU_CHUNK = 128
SGU_LN_EPS = 1e-5
RW_GN_EPS = 64e-5
RW_LORA_W, RW_LORA_A, RW_LORA_G = 64, 64, 128
N_GROUPS = 8
EXPERTS_PER_GROUP = 8
N_EXPERTS = N_GROUPS * EXPERTS_PER_GROUP
D_EXPERT = 256
PAIRS_PER_GROUP = EXPERTS_PER_GROUP * (EXPERTS_PER_GROUP - 1) // 2
N_BUCKETS = N_GROUPS * PAIRS_PER_GROUP
LANES = 128
VMEM_LIMIT = 56 * 1024 * 1024

ROW_BLOCK = 512
MOE_ROWS = 128
MOE_TOK = 512


def _dot(a, b):
    return jnp.dot(a.astype(BF16), b.astype(BF16), preferred_element_type=F32)


def _dot_nt(a, b):
    return lax.dot_general(a.astype(BF16), b.astype(BF16), (((1,), (1,)), ((), ())),
                           preferred_element_type=F32)


def _dot_tn(a, b):
    return lax.dot_general(a.astype(BF16), b.astype(BF16), (((0,), (0,)), ((), ())),
                           preferred_element_type=F32)


def _split2(x):
    hi = x.astype(BF16)
    lo = (x - hi.astype(F32)).astype(BF16)
    return hi, lo


def _dot_x_exact(x, m):
    hi, lo = _split2(x)
    return (jnp.dot(hi, m, preferred_element_type=F32) + jnp.dot(lo, m, preferred_element_type=F32))


def _dot_exact_x(m, x):
    hi, lo = _split2(x)
    return (jnp.dot(m, hi, preferred_element_type=F32) + jnp.dot(m, lo, preferred_element_type=F32))


def _sigmoid(x):
    return 1.0 / (1.0 + jnp.exp(-x))


def _silu(x):
    return x * _sigmoid(x)


def _softplus(x):
    return jnp.maximum(x, 0.0) + jnp.log(1.0 + jnp.exp(-jnp.abs(x)))


def _head_masks():
    lane = lax.broadcasted_iota(jnp.int32, (1, GROUP_W), 1)
    return [((lane >> 6) == h).astype(F32) for h in range(N_HEADS)]


def _tile_masks():
    ri = lax.broadcasted_iota(jnp.int32, (TILE, TILE), 0)
    ci = lax.broadcasted_iota(jnp.int32, (TILE, TILE), 1)
    same = (ri >> 6) == (ci >> 6)
    return same & (ri > ci), same & (ri >= ci), (ri == ci).astype(F32)


def _block_diag_mask():
    ri = lax.broadcasted_iota(jnp.int32, (GROUP_W, GROUP_W), 0)
    ci = lax.broadcasted_iota(jnp.int32, (GROUP_W, GROUP_W), 1)
    return ((ri >> 6) == (ci >> 6)).astype(F32)


def _merge_masks():
    ri = lax.broadcasted_iota(jnp.int32, (TILE, TILE), 0)
    ci = lax.broadcasted_iota(jnp.int32, (TILE, TILE), 1)
    return [((ri >> (l + 1)) == (ci >> (l + 1))) & (((ri >> l) & 1) == 1) & (((ci >> l) & 1) == 0) for l in range(6)]


def _unit_lower_inverses(a, eye, chains):
    masks = _merge_masks()
    d = {c: eye - jnp.where(masks[0], a[c], 0.0) for c in chains}
    for l in range(1, 6):
        f = {c: _dot(jnp.where(masks[l], a[c], 0.0), d[c]) for c in chains}
        d = {c: d[c] - _dot(d[c], f[c]) for c in chains}
    return d


def _seg_sum(x, seg):
    return jnp.dot(x.astype(BF16), seg, preferred_element_type=F32)


def _cparams(sem):
    return pltpu.CompilerParams(dimension_semantics=sem, vmem_limit_bytes=VMEM_LIMIT)


def _ada_kernel(c_ref, w_ref, b_ref, o_ref):
    c = c_ref[...]
    ca = _silu(c)
    chi, clo = _split2(ca)
    w = w_ref[...]
    whi, wlo = _split2(w)
    acc = jnp.dot(chi, whi, preferred_element_type=F32)
    acc += jnp.dot(clo, whi, preferred_element_type=F32)
    acc += jnp.dot(chi, wlo, preferred_element_type=F32)
    o_ref[...] = acc + b_ref[...]


def _ada(c, ada_w, ada_b):
    depth, d, d6 = ada_w.shape
    bn = c.shape[0]
    nj = d6 // d
    return pl.pallas_call(
        _ada_kernel,
        out_shape=jax.ShapeDtypeStruct((depth, bn, d6), F32),
        grid=(depth, nj),
        in_specs=[pl.BlockSpec((bn, d), lambda l, j: (0, 0)),
                  pl.BlockSpec((None, d, d), lambda l, j: (l, 0, j)),
                  pl.BlockSpec((None, 1, d), lambda l, j: (l, 0, j))],
        out_specs=pl.BlockSpec((None, bn, d), lambda l, j: (l, 0, j)),
        compiler_params=_cparams(("arbitrary", "arbitrary")),
        name="ada_mod",
    )(c, ada_w, ada_b.reshape(depth, 1, d6))


def _in_proj_kernel(x_ref, g_ref, sh_ref, sc_ref, w_ref, wab_ref, pg_ref, psc_ref, prw_ref, pab_ref, abt_ref):
    x = x_ref[...]
    y = x * lax.rsqrt(jnp.mean(x * x, -1, keepdims=True) + NORM_EPS) * g_ref[...]
    h = (y * (1.0 + sc_ref[0]) + sh_ref[0]).astype(BF16)
    o = 0
    for ref in (pg_ref, psc_ref, prw_ref, pab_ref):
        w = ref.shape[1]
        ref[...] = jnp.dot(h, w_ref[:, o:o + w], preferred_element_type=F32)
        o += w
    abt_ref[...] = lax.dot_general(wab_ref[...], h, (((1,), (1,)), ((), ())), preferred_element_type=F32)


def _in_proj(x, g, shift, scale, w_r, w_abt, seq):
    t, d = x.shape
    tm = ROW_BLOCK
    per_b = seq // tm
    widths = (4 * GROUP_W, 5 * GROUP_W, 4 * GROUP_W, LANES)
    bspec = pl.BlockSpec((1, 1, d), lambda i: (i // per_b, 0, 0))
    return pl.pallas_call(
        _in_proj_kernel,
        out_shape=tuple(jax.ShapeDtypeStruct((t, w), F32) for w in widths) + (jax.ShapeDtypeStruct((8, t), F32),),
        grid=(t // tm,),
        in_specs=[pl.BlockSpec((tm, d), lambda i: (i, 0)),
                  pl.BlockSpec((1, d), lambda i: (0, 0)),
                  bspec, bspec,
                  pl.BlockSpec(w_r.shape, lambda i: (0, 0)),
                  pl.BlockSpec(w_abt.shape, lambda i: (0, 0))],
        out_specs=tuple(pl.BlockSpec((tm, w), lambda i: (i, 0)) for w in widths) + (pl.BlockSpec((8, tm), lambda i: (0, i)),),
        compiler_params=_cparams(("arbitrary",)),
        name="in_proj",
    )(x, g, shift, scale, w_r, w_abt)


def _gdn_kernel(p_ref, ab_ref, abt_ref, cw_ref, alog_ref, dtb_ref, alogt_ref, dtbt_ref, ng_ref,
                eg_ref, eb_ref, seg_ref, tri_ref, trit_ref, full_ref,
                o_ref,
                xbuf, s_ref, q_s, k_s, kb_s, rhs_s, qd_s, kt_s, cd_s, gc4_s, gct_s, u_s, w_s, snap_s):
    rows = p_ref.shape[0]
    j = pl.program_id(1)

    @pl.when(j == 0)
    def _():
        xbuf[0:8, :] = jnp.zeros((8, xbuf.shape[1]), F32)
        s_ref[...] = jnp.zeros(s_ref.shape, F32)

    xbuf[8:8 + rows, :] = p_ref[:, 0:3 * GROUP_W]
    acc = cw_ref[3:4, :] * xbuf[8:8 + rows, :]
    for tap in range(3):
        acc = acc + cw_ref[tap:tap + 1, :] * xbuf[5 + tap:5 + tap + rows, :]
    xbuf[0:8, :] = xbuf[rows:rows + 8, :]
    qkv = _silu(acc)
    seg = seg_ref[...]
    q = qkv[:, 0:GROUP_W]
    k = qkv[:, GROUP_W:2 * GROUP_W]
    v = qkv[:, 2 * GROUP_W:3 * GROUP_W]
    q = q * lax.rsqrt(_seg_sum(q * q, seg) + 1e-6) * (HEAD_DIM ** -0.5)
    k = k * lax.rsqrt(_seg_sum(k * k, seg) + 1e-6)

    ab = ab_ref[...]
    g = -jnp.exp(alog_ref[...]) * _softplus(ab + dtb_ref[...])
    beta = jnp.dot(_sigmoid(ab).astype(BF16), eb_ref[...], preferred_element_type=F32)
    tsl = [slice(t * TILE, (t + 1) * TILE) for t in range(rows // TILE)]
    gc4 = jnp.concatenate([_dot_exact_x(tri_ref[...], g[sl]) for sl in tsl], axis=0)
    gl4 = jnp.concatenate([_dot_exact_x(full_ref[...], g[sl]) for sl in tsl], axis=0)
    gc4_s[...] = gc4
    gc = _dot_x_exact(gc4, eg_ref[...])
    gl = _dot_x_exact(gl4, eg_ref[...])
    egc = jnp.exp(gc)
    kb = k * beta
    q_s[...] = q
    k_s[...] = k
    kb_s[...] = kb
    rhs_s[:, 0:GROUP_W] = v * beta
    rhs_s[:, GROUP_W:2 * GROUP_W] = kb * egc
    qd_s[...] = q * egc
    kt_s[...] = k * jnp.exp(gl - gc)
    cd_s[...] = jnp.exp(gl)
    abt = abt_ref[...]
    gt = -jnp.exp(alogt_ref[...]) * _softplus(abt + dtbt_ref[...])
    for t in range(rows // TILE):
        gct_s[t] = _dot_x_exact(gt[:, tsl[t]], trit_ref[...])

    hm = _head_masks()
    strict, incl, eye = _tile_masks()
    bd = _block_diag_mask()
    ng = ng_ref[...]
    tiles = range(rows // TILE)
    heads = range(N_HEADS)
    chains = [(t, h) for t in tiles for h in heads]
    rsl = [slice(t * TILE, (t + 1) * TILE) for t in tiles]

    x, attn = {}, {}
    for t in tiles:
        kt_ = k_s[rsl[t], :].astype(BF16)
        kbt, qt = kb_s[rsl[t], :], q_s[rsl[t], :]
        gct_t = gct_s[t]
        gc4_t = gc4_s[rsl[t], :]
        for h in heads:
            dec = jnp.exp(jnp.where(incl, gc4_t[:, h:h + 1] - gct_t[h:h + 1, :], -jnp.inf))
            x[t, h] = jnp.where(strict, _dot_nt(kbt * hm[h], kt_) * dec, 0.0)
            attn[t, h] = (_dot_nt(qt * hm[h], kt_) * dec).astype(BF16)
    p = _unit_lower_inverses(x, eye, chains)
    for t in tiles:
        rhs = rhs_s[rsl[t], :].astype(BF16)
        u = jnp.zeros((TILE, GROUP_W), F32)
        w = jnp.zeros((TILE, GROUP_W), F32)
        for h in heads:
            sol = _dot(p[t, h], rhs)
            u = u + hm[h] * sol[:, 0:GROUP_W]
            w = w + hm[h] * sol[:, GROUP_W:2 * GROUP_W]
        u_s[rsl[t], :] = u
        w_s[rsl[t], :] = w

    chunks = range(rows // CHUNK)
    csl = [slice(c * CHUNK, (c + 1) * CHUNK) for c in chunks]
    pq = [_dot_tn(kt_s[csl[c], :], jnp.concatenate([w_s[csl[c], :], u_s[csl[c], :]], axis=1)) for c in chunks]
    pmat = [(bd * pq[c][:, 0:GROUP_W]).astype(BF16) for c in chunks]
    s = s_ref[...]
    for c in chunks:
        snap_s[c] = s.astype(BF16)
        s = s * cd_s[c * CHUNK:c * CHUNK + 1, :] - _dot(pmat[c], s) + bd * pq[c][:, GROUP_W:2 * GROUP_W]
    s_ref[...] = s
    for c in chunks:
        ws = jnp.dot(jnp.concatenate([w_s[csl[c], :], qd_s[csl[c], :]], axis=0).astype(BF16), snap_s[c],
                     preferred_element_type=F32)
        u_s[csl[c], :] = u_s[csl[c], :] - ws[0:CHUNK]
        w_s[csl[c], :] = ws[CHUNK:2 * CHUNK]

    for t in tiles:
        vn = u_s[rsl[t], :].astype(BF16)
        o = w_s[rsl[t], :]
        for h in heads:
            o = o + hm[h] * _dot(attn[t, h], vn)
        o = o * lax.rsqrt(_seg_sum(o * o, seg) * (1.0 / HEAD_DIM) + NORM_EPS) * ng
        o_ref[rsl[t], :] = (o * _silu(p_ref[rsl[t], 3 * GROUP_W:4 * GROUP_W])).astype(o_ref.dtype)


def _chunk_mats():
    ri = jnp.arange(TILE)[:, None]
    ci = jnp.arange(TILE)[None, :]
    same = (ri // CHUNK) == (ci // CHUNK)
    tri = (same & (ci <= ri)).astype(BF16)
    return tri, tri.T, same.astype(BF16)


def _expand_mats():
    lane = jnp.arange(LANES)[:, None]
    col = jnp.arange(GROUP_W)[None, :]
    eg = (lane == col // HEAD_DIM).astype(BF16)
    eb = (lane == N_HEADS + col // HEAD_DIM).astype(BF16)
    seg = ((jnp.arange(GROUP_W)[:, None] // HEAD_DIM) == (col // HEAD_DIM)).astype(BF16)
    return eg, eb, seg


def _pad_lanes(v, n=LANES):
    return jnp.zeros((1, n), F32).at[0, :v.shape[0]].set(v.astype(F32))


def _gdn(pg, pab, abt, conv_w, a_log, dt_bias, norm_g, bn, seq):
    t = pg.shape[0]
    rows = ROW_BLOCK
    nb = seq // rows
    eg, eb, seg = _expand_mats()
    tri, trit, full = _chunk_mats()
    alog_t = jnp.zeros((8, rows), F32).at[:N_HEADS].set(jnp.broadcast_to(a_log[:, None], (N_HEADS, rows)))
    dtb_t = jnp.zeros((8, rows), F32).at[:N_HEADS].set(jnp.broadcast_to(dt_bias[:, None], (N_HEADS, rows)))
    consts = (conv_w.astype(F32), _pad_lanes(a_log), _pad_lanes(dt_bias), alog_t, dtb_t,
              jnp.tile(norm_g.astype(F32), N_HEADS)[None, :], eg, eb, seg, tri, trit, full)
    rowmap = lambda b, j: (b * nb + j, 0)
    return pl.pallas_call(
        _gdn_kernel,
        out_shape=jax.ShapeDtypeStruct((t, GROUP_W), BF16),
        grid=(bn, nb),
        in_specs=[pl.BlockSpec((rows, 4 * GROUP_W), rowmap),
                  pl.BlockSpec((rows, LANES), rowmap),
                  pl.BlockSpec((8, rows), lambda b, j: (0, b * nb + j))]
                 + [pl.BlockSpec(c.shape, lambda b, j: (0, 0)) for c in consts],
        out_specs=pl.BlockSpec((rows, GROUP_W), rowmap),
        scratch_shapes=[pltpu.VMEM((rows + 8, 3 * GROUP_W), F32),
                        pltpu.VMEM((GROUP_W, GROUP_W), F32),
                        pltpu.VMEM((rows, GROUP_W), F32), pltpu.VMEM((rows, GROUP_W), F32),
                        pltpu.VMEM((rows, GROUP_W), F32), pltpu.VMEM((rows, 2 * GROUP_W), F32),
                        pltpu.VMEM((rows, GROUP_W), F32), pltpu.VMEM((rows, GROUP_W), F32),
                        pltpu.VMEM((rows, GROUP_W), F32), pltpu.VMEM((rows, LANES), F32),
                        pltpu.VMEM((rows // TILE, 8, TILE), F32),
                        pltpu.VMEM((rows, GROUP_W), F32), pltpu.VMEM((rows, GROUP_W), F32),
                        pltpu.VMEM((rows // CHUNK, GROUP_W, GROUP_W), BF16)],
        compiler_params=_cparams(("arbitrary", "arbitrary")),
        name="gdn_mixer",
    )(pg, pab, abt, *consts)


def _sgu_conv_kernel(p_ref, lng_ref, lnb_ref, ws_ref, bs_ref, cw_ref, o_ref, xbuf):
    rows = p_ref.shape[0]
    j = pl.program_id(1)

    @pl.when(j == 0)
    def _():
        xbuf[0:8, :] = jnp.zeros((8, GROUP_W), F32)

    u = jax.nn.gelu(p_ref[:, 0:GROUP_W])
    vf = jax.nn.gelu(p_ref[:, GROUP_W:2 * GROUP_W])
    mean = jnp.mean(vf, -1, keepdims=True)
    var = jnp.mean(jnp.square(vf - mean), -1, keepdims=True)
    v = (vf - mean) * lax.rsqrt(var + SGU_LN_EPS) * lng_ref[...] + lnb_ref[...]
    hm = _head_masks()
    ri = lax.broadcasted_iota(jnp.int32, (SGU_CHUNK, SGU_CHUNK), 0)
    ci = lax.broadcasted_iota(jnp.int32, (SGU_CHUNK, SGU_CHUNK), 1)
    ws = [jnp.where(ri >= ci, ws_ref[h], 0.0).astype(BF16) for h in range(N_HEADS)]
    bs = bs_ref[...]
    for c in range(rows // SGU_CHUNK):
        cs = slice(c * SGU_CHUNK, (c + 1) * SGU_CHUNK)
        vc = v[cs].astype(BF16)
        mixed = bs
        for h in range(N_HEADS):
            mixed = mixed + hm[h] * jnp.dot(ws[h], vc, preferred_element_type=F32)
        o_ref[cs, 0:GROUP_W] = (u[cs] * mixed).astype(o_ref.dtype)

    xbuf[8:8 + rows, :] = p_ref[:, 3 * GROUP_W:4 * GROUP_W] * p_ref[:, 4 * GROUP_W:5 * GROUP_W]
    acc = cw_ref[2:3, :] * xbuf[8:8 + rows, :]
    for tap in range(2):
        acc = acc + cw_ref[tap:tap + 1, :] * xbuf[6 + tap:6 + tap + rows, :]
    xbuf[0:8, :] = xbuf[rows:rows + 8, :]
    o_ref[:, GROUP_W:2 * GROUP_W] = (p_ref[:, 2 * GROUP_W:3 * GROUP_W] * acc).astype(o_ref.dtype)


def _sgu_conv(psc, ln_g, ln_b, w_s, b_s, conv_w, bn, seq):
    t = psc.shape[0]
    rows = ROW_BLOCK
    nb = seq // rows
    bs_exp = jnp.repeat(b_s.T.astype(F32), HEAD_DIM, axis=1)
    consts = (ln_g[None, :].astype(F32), ln_b[None, :].astype(F32), w_s.astype(F32), bs_exp, conv_w.astype(F32))
    rowmap = lambda b, j: (b * nb + j, 0)
    return pl.pallas_call(
        _sgu_conv_kernel,
        out_shape=jax.ShapeDtypeStruct((t, 2 * GROUP_W), BF16),
        grid=(bn, nb),
        in_specs=[pl.BlockSpec((rows, 5 * GROUP_W), rowmap)]
                 + [pl.BlockSpec(c.shape, lambda b, j, n=c.ndim: (0,) * n) for c in consts],
        out_specs=pl.BlockSpec((rows, 2 * GROUP_W), rowmap),
        scratch_shapes=[pltpu.VMEM((rows + 8, GROUP_W), F32)],
        compiler_params=_cparams(("arbitrary", "arbitrary")),
        name="sgu_conv_mixer",
    )(psc, *consts)


def _rwkv_kernel(p_ref, mu_ref, w0_ref, wup_ref, a0_ref, aup_ref, gup_ref, kk_ref, ka_ref, rk_ref, gng_ref, gnb_ref,
                 seg_ref, tri_ref, full_ref,
                 o_ref,
                 prev, s_ref, at_s, bt_s, kt_s, rt_s, v_s, btl_s, ktl_s, gam_s, bon_s, gate_s, wa_s, u_s, y_s, snap_s):
    rows = p_ref.shape[0]
    j = pl.program_id(1)

    @pl.when(j == 0)
    def _():
        prev[...] = jnp.zeros(prev.shape, F32)
        s_ref[...] = jnp.zeros(s_ref.shape, F32)

    prev[8:8 + rows, :] = p_ref[...]
    p = p_ref[...]
    p = p + (prev[7:7 + rows, :] - p) * mu_ref[...]
    prev[0:8, :] = prev[rows:rows + 8, :]
    g_w = GROUP_W
    r = p[:, 0:g_w]
    k = p[:, g_w:2 * g_w]
    v = p[:, 2 * g_w:3 * g_w]
    o = 3 * g_w
    xw = p[:, o:o + RW_LORA_W]
    xa = p[:, o + RW_LORA_W:o + RW_LORA_W + RW_LORA_A]
    xg = p[:, o + RW_LORA_W + RW_LORA_A:o + RW_LORA_W + RW_LORA_A + RW_LORA_G]
    w_log = -_softplus(-(w0_ref[...] + _dot(jnp.tanh(xw), wup_ref[...]))) - 0.5
    lw = -jnp.exp(w_log)
    a = _sigmoid(a0_ref[...] + _dot(xa, aup_ref[...]))
    gate_s[...] = _dot(_sigmoid(xg), gup_ref[...])
    seg = seg_ref[...]
    kk = k * kk_ref[...]
    kk = kk * lax.rsqrt(_seg_sum(kk * kk, seg) + 1e-12)
    k_mod = k * (1.0 + (a - 1.0) * ka_ref[...])
    bon_s[...] = _seg_sum(r * k_mod * rk_ref[...], seg) * v
    tsl = [slice(t * TILE, (t + 1) * TILE) for t in range(rows // TILE)]
    cl = jnp.concatenate([_dot_exact_x(tri_ref[...], lw[sl]) for sl in tsl], axis=0)
    ct = jnp.concatenate([_dot_exact_x(full_ref[...], lw[sl]) for sl in tsl], axis=0)
    e_neg = jnp.exp(-cl)
    e_tail = jnp.exp(ct - cl)
    zb = kk * a
    at_s[...] = -kk * jnp.exp(cl - lw)
    bt_s[...] = zb * e_neg
    kt_s[...] = k_mod * e_neg
    rt_s[...] = r * jnp.exp(cl)
    v_s[...] = v
    btl_s[...] = zb * e_tail
    ktl_s[...] = k_mod * e_tail
    gam_s[...] = jnp.exp(ct)

    hm = _head_masks()
    strict, incl, eye = _tile_masks()
    bd = _block_diag_mask()
    gng, gnb = gng_ref[...], gnb_ref[...]

    tiles = range(rows // TILE)
    heads = range(N_HEADS)
    chains = [(t, h) for t in tiles for h in heads]
    rsl = [slice(t * TILE, (t + 1) * TILE) for t in tiles]

    x, lak, mrb, mrk = {}, {}, {}, {}
    for t in tiles:
        at, rt = at_s[rsl[t], :], rt_s[rsl[t], :]
        rhs_nt = jnp.concatenate([bt_s[rsl[t], :], kt_s[rsl[t], :]], axis=0).astype(BF16)
        for h in heads:
            sc = _dot_nt(jnp.concatenate([at * hm[h], rt * hm[h]], axis=0), rhs_nt)
            x[t, h] = jnp.where(strict, -sc[0:TILE, 0:TILE], 0.0)
            lak[t, h] = jnp.where(strict, sc[0:TILE, TILE:2 * TILE], 0.0).astype(BF16)
            mrb[t, h] = jnp.where(incl, sc[TILE:2 * TILE, 0:TILE], 0.0).astype(BF16)
            mrk[t, h] = jnp.where(incl, sc[TILE:2 * TILE, TILE:2 * TILE], 0.0).astype(BF16)
    p = _unit_lower_inverses(x, eye, chains)
    for t in tiles:
        at = at_s[rsl[t], :]
        vt = v_s[rsl[t], :].astype(BF16)
        wa = jnp.zeros((TILE, GROUP_W), F32)
        u0 = jnp.zeros((TILE, GROUP_W), F32)
        y0 = jnp.zeros((TILE, GROUP_W), F32)
        for h in heads:
            sol = _dot(p[t, h], jnp.concatenate([at, _dot(lak[t, h], vt)], axis=1))
            wa = wa + hm[h] * sol[:, 0:GROUP_W]
            u0 = u0 + hm[h] * sol[:, GROUP_W:2 * GROUP_W]
            y0 = y0 + hm[h] * _dot(mrk[t, h], vt)
        wa_s[rsl[t], :] = wa
        u_s[rsl[t], :] = u0
        y_s[rsl[t], :] = y0

    chunks = range(rows // CHUNK)
    csl = [slice(c * CHUNK, (c + 1) * CHUNK) for c in chunks]
    pmat = [(bd * _dot_tn(wa_s[csl[c], :], btl_s[csl[c], :])).astype(BF16) for c in chunks]
    qmat = [bd * _dot_tn(jnp.concatenate([u_s[csl[c], :], v_s[csl[c], :]], axis=0),
                         jnp.concatenate([btl_s[csl[c], :], ktl_s[csl[c], :]], axis=0)) for c in chunks]
    s = s_ref[...]
    for c in chunks:
        snap_s[c] = s.astype(BF16)
        s = s * gam_s[c * CHUNK:c * CHUNK + 1, :] + _dot(s, pmat[c]) + qmat[c]
    s_ref[...] = s
    for c in chunks:
        us = _dot_nt(jnp.concatenate([wa_s[csl[c], :], rt_s[csl[c], :]], axis=0), snap_s[c])
        u_s[csl[c], :] = u_s[csl[c], :] + us[0:CHUNK]
        y_s[csl[c], :] = y_s[csl[c], :] + us[CHUNK:2 * CHUNK]

    for t in tiles:
        u = u_s[rsl[t], :].astype(BF16)
        y = y_s[rsl[t], :]
        for h in heads:
            y = y + hm[h] * _dot(mrb[t, h], u)
        mean = _seg_sum(y, seg) * (1.0 / HEAD_DIM)
        yc = y - mean
        var = _seg_sum(yc * yc, seg) * (1.0 / HEAD_DIM)
        yn = yc * lax.rsqrt(var + RW_GN_EPS) * gng + gnb
        o_ref[rsl[t], :] = ((yn + bon_s[rsl[t], :]) * gate_s[rsl[t], :]).astype(o_ref.dtype)


def _rwkv(prw, mu, w0, w_up, a0, a_up, g_up, k_k, k_a, r_k, gn_g, gn_b, bn, seq):
    t = prw.shape[0]
    rows = ROW_BLOCK
    nb = seq // rows
    _, _, seg = _expand_mats()
    tri, _, full = _chunk_mats()
    row = lambda x: x.reshape(1, -1).astype(F32)
    consts = (row(mu), row(w0), w_up.astype(BF16), row(a0), a_up.astype(BF16), g_up.astype(BF16),
              row(k_k), row(k_a), row(r_k), row(gn_g), row(gn_b), seg, tri, full)
    rowmap = lambda b, j: (b * nb + j, 0)
    big = lambda: pltpu.VMEM((rows, GROUP_W), F32)
    return pl.pallas_call(
        _rwkv_kernel,
        out_shape=jax.ShapeDtypeStruct((t, GROUP_W), BF16),
        grid=(bn, nb),
        in_specs=[pl.BlockSpec((rows, 4 * GROUP_W), rowmap)]
                 + [pl.BlockSpec(c.shape, lambda b, j: (0, 0)) for c in consts],
        out_specs=pl.BlockSpec((rows, GROUP_W), rowmap),
        scratch_shapes=[pltpu.VMEM((rows + 8, 4 * GROUP_W), F32), pltpu.VMEM((GROUP_W, GROUP_W), F32)]
                       + [big() for _ in range(13)] + [pltpu.VMEM((rows // CHUNK, GROUP_W, GROUP_W), BF16)],
        compiler_params=_cparams(("arbitrary", "arbitrary")),
        name="rwkv7_mixer",
    )(prw, *consts)


def _out_router_kernel(x_ref, oa_ref, obc_ref, od_ref, wo_ref, gt_ref, g_ref, sh_ref, sc_ref,
                       wrh_ref, wrl_ref, br_ref, tri_ref,
                       xo_ref, hf_ref, route_ref, cnt_ref, carry):
    i = pl.program_id(0)

    @pl.when(i == 0)
    def _():
        carry[...] = jnp.zeros(carry.shape, F32)

    g_w = GROUP_W
    tm, d_model = x_ref.shape
    sub = TILE
    subs = [slice(s * sub, (s + 1) * sub) for s in range(tm // sub)]
    lane = lax.broadcasted_iota(jnp.int32, (sub, LANES), 1)
    lanef = lane.astype(F32)
    lane2 = lax.broadcasted_iota(jnp.int32, (sub, 2 * LANES), 1).astype(F32)
    big = jnp.float32(1e9)
    ninf = jnp.float32(-jnp.inf)
    is_g = (lane >= N_EXPERTS) & (lane < N_EXPERTS + N_GROUPS)

    def out_proj(rs):
        mixed = jnp.dot(oa_ref[rs, :], wo_ref[0:g_w, :], preferred_element_type=F32)
        mixed += jnp.dot(obc_ref[rs, :], wo_ref[g_w:3 * g_w, :], preferred_element_type=F32)
        mixed += jnp.dot(od_ref[rs, :], wo_ref[3 * g_w:4 * g_w, :], preferred_element_type=F32)
        return mixed

    def residual_norm(rs, mixed):
        x = x_ref[rs, :] + gt_ref[0] * mixed
        xo_ref[rs, :] = x
        y = x * lax.rsqrt(jnp.mean(x * x, -1, keepdims=True) + NORM_EPS) * g_ref[...]
        hf = y * (1.0 + sc_ref[0]) + sh_ref[0]
        hf_ref[rs, 0:d_model] = hf
        return _split2(hf)

    def logits(hh, hl):
        return (jnp.dot(hh, wrh_ref[...], preferred_element_type=F32) + jnp.dot(hl, wrh_ref[...], preferred_element_type=F32)
                + jnp.dot(hh, wrl_ref[...], preferred_element_type=F32) + br_ref[...])

    def route(rs, lg):
        gl = jnp.where(is_g, lg, ninf)
        gmax = jnp.max(gl, -1, keepdims=True)
        gsel = jnp.min(jnp.where(gl == gmax, lanef - N_EXPERTS, big), -1, keepdims=True)
        p_group = 1.0 / jnp.sum(jnp.where(is_g, jnp.exp(gl - gmax), 0.0), -1, keepdims=True)
        in_grp = (lane < N_EXPERTS) & ((lane >> 3).astype(F32) == gsel)
        el = jnp.where(in_grp, lg, ninf)
        v1 = jnp.max(el, -1, keepdims=True)
        i1 = jnp.min(jnp.where(el == v1, lanef, big), -1, keepdims=True)
        el2 = jnp.where(lanef == i1, ninf, el)
        v2 = jnp.max(el2, -1, keepdims=True)
        i2 = jnp.min(jnp.where(el2 == v2, lanef, big), -1, keepdims=True)
        e21 = jnp.exp(v2 - v1)
        g1 = p_group / (1.0 + e21)
        g2 = p_group * e21 / (1.0 + e21)
        first_lo = i1 < i2
        a = jnp.where(first_lo, i1, i2) - gsel * EXPERTS_PER_GROUP
        b = jnp.where(first_lo, i2, i1) - gsel * EXPERTS_PER_GROUP
        bucket = gsel * PAIRS_PER_GROUP + a * (2 * EXPERTS_PER_GROUP - 1 - a) * 0.5 + (b - a - 1.0)
        g_lo = jnp.where(first_lo, g1, g2)
        g_hi = jnp.where(first_lo, g2, g1)
        hf_ref[rs, d_model:d_model + LANES] = jnp.where(lane == 0, g_lo, jnp.where(lane == 1, g_hi, 0.0))
        return bucket, jnp.where(lane2 == bucket, 1.0, 0.0)

    mixed = [out_proj(rs) for rs in subs]
    halves = [residual_norm(rs, m) for rs, m in zip(subs, mixed)]
    lgs = [logits(hh, hl) for hh, hl in halves]
    routed = [route(rs, lg) for rs, lg in zip(subs, lgs)]
    oh = jnp.concatenate([r[1] for r in routed], axis=0)
    total = jnp.dot(tri_ref[...], oh.astype(BF16), preferred_element_type=F32) + carry[...]
    rank = jnp.sum(oh * total, -1, keepdims=True)
    carry[...] = carry[...] + jnp.sum(oh, axis=0, keepdims=True)
    cnt_ref[...] = carry[...]
    for s, rs in enumerate(subs):
        route_ref[rs, :] = jnp.where(lane == 0, routed[s][0], jnp.where(lane == 1, rank[rs], 0.0))


def _out_router(x, oa, obc, od, w_out, gt, g, shift, scale, wr_hi, wr_lo, b_r, seq):
    t, d = x.shape
    tm = ROW_BLOCK
    per_b = seq // tm
    tri = (jnp.arange(tm)[:, None] > jnp.arange(tm)[None, :]).astype(BF16)
    bspec = pl.BlockSpec((1, 1, d), lambda i: (i // per_b, 0, 0))
    full = lambda a: pl.BlockSpec(a.shape, lambda i: (0,) * a.ndim)
    return pl.pallas_call(
        _out_router_kernel,
        out_shape=(jax.ShapeDtypeStruct((t, d), F32), jax.ShapeDtypeStruct((t, d + LANES), F32),
                   jax.ShapeDtypeStruct((t, LANES), F32), jax.ShapeDtypeStruct((1, 2 * LANES), F32)),
        grid=(t // tm,),
        in_specs=[pl.BlockSpec((tm, d), lambda i: (i, 0)),
                  pl.BlockSpec((tm, GROUP_W), lambda i: (i, 0)),
                  pl.BlockSpec((tm, 2 * GROUP_W), lambda i: (i, 0)),
                  pl.BlockSpec((tm, GROUP_W), lambda i: (i, 0)),
                  full(w_out), bspec, full(g), bspec, bspec, full(wr_hi), full(wr_lo), full(b_r), full(tri)],
        out_specs=(pl.BlockSpec((tm, d), lambda i: (i, 0)), pl.BlockSpec((tm, d + LANES), lambda i: (i, 0)),
                   pl.BlockSpec((tm, LANES), lambda i: (i, 0)), pl.BlockSpec((1, 2 * LANES), lambda i: (0, 0))),
        scratch_shapes=[pltpu.VMEM((1, 2 * LANES), F32)],
        compiler_params=_cparams(("arbitrary",)),
        name="out_proj_router",
    )(x, oa, obc, od, w_out, gt, g, shift, scale, wr_hi, wr_lo, b_r, tri)


def _row_copy(src, src_row, dst, dst_row, sem):
    return pltpu.make_async_copy(src.at[pl.ds(src_row, 1), :], dst.at[pl.ds(dst_row, 1), :], sem)


DMA_UNROLL = 16


def _dest_row(ps_ref, rt_ref, r):
    return ps_ref[rt_ref[0, 0, 2 * r]] + rt_ref[0, 0, 2 * r + 1]


STAGE_SLOTS = 3


def _dispatch_kernel(ps_ref, rt_ref, hf_ref, xs_in_ref, xs_ref, stage, stage_sems, row_sems):
    del xs_in_ref
    i = pl.program_id(0)
    last = pl.num_programs(0) - 1
    n_tok = stage.shape[1]
    slot = lax.rem(i, STAGE_SLOTS)

    def stage_copy(blk, s):
        return pltpu.make_async_copy(hf_ref.at[pl.ds(blk * n_tok, n_tok), :], stage.at[s], stage_sems.at[s])

    def rows_done(s):
        pltpu.make_async_copy(stage.at[s], xs_ref.at[pl.ds(0, n_tok), :], row_sems.at[s]).wait()

    @pl.when(i == 0)
    def _():
        stage_copy(0, 0).start()

    @pl.when(i < last)
    def _():
        stage_copy(i + 1, lax.rem(i + 1, STAGE_SLOTS)).start()

    stage_copy(i, slot).wait()

    def issue(g, carry):
        for uu in range(DMA_UNROLL):
            r = g * DMA_UNROLL + uu
            _row_copy(stage.at[slot], r, xs_ref, _dest_row(ps_ref, rt_ref, r), row_sems.at[slot]).start()
        return carry

    lax.fori_loop(0, n_tok // DMA_UNROLL, issue, 0)

    @pl.when(i > 0)
    def _():
        rows_done(lax.rem(i + STAGE_SLOTS - 1, STAGE_SLOTS))

    @pl.when(i == last)
    def _():
        rows_done(slot)


def _dispatch(pad_start, rt3, hf, p_rows):
    t, d = hf.shape
    td = MOE_TOK
    xs0 = jnp.zeros((p_rows, d), F32)
    grid_spec = pltpu.PrefetchScalarGridSpec(
        num_scalar_prefetch=1,
        grid=(t // td,),
        in_specs=[pl.BlockSpec((1, 1, 2 * td), lambda i, ps: (i, 0, 0), memory_space=pltpu.SMEM),
                  pl.BlockSpec(memory_space=pl.ANY),
                  pl.BlockSpec(memory_space=pl.ANY)],
        out_specs=pl.BlockSpec(memory_space=pl.ANY),
        scratch_shapes=[pltpu.VMEM((STAGE_SLOTS, td, d), F32), pltpu.SemaphoreType.DMA((STAGE_SLOTS,)),
                        pltpu.SemaphoreType.DMA((STAGE_SLOTS,))],
    )
    return pl.pallas_call(
        _dispatch_kernel,
        out_shape=jax.ShapeDtypeStruct((p_rows, d), F32),
        grid_spec=grid_spec,
        input_output_aliases={3: 0},
        compiler_params=_cparams(("arbitrary",)),
        name="moe_dispatch",
    )(pad_start, rt3, hf, xs0)


def _expert_mlp(xb, wg_b, wu_b, wd_b):
    hid = _silu(jnp.dot(xb, wg_b[...], preferred_element_type=F32)) * jnp.dot(xb, wu_b[...], preferred_element_type=F32)
    return jnp.dot(hid.astype(BF16), wd_b[...], preferred_element_type=F32)


def _cast_kernel(a_ref, b_ref, c_ref, ao_ref, bo_ref, co_ref):
    ao_ref[...] = a_ref[...].astype(BF16)
    bo_ref[...] = b_ref[...].astype(BF16)
    co_ref[...] = c_ref[...].astype(BF16)


def _cast_expert_weights(w_gate, w_up, w_down):
    depth, n_e = w_gate.shape[:2]
    per_step = 2
    flat = [w.reshape((depth * n_e,) + w.shape[2:]) for w in (w_gate, w_up, w_down)]
    spec = lambda w: pl.BlockSpec((per_step,) + w.shape[1:], lambda i: (i, 0, 0))
    outs = pl.pallas_call(
        _cast_kernel,
        out_shape=tuple(jax.ShapeDtypeStruct(w.shape, BF16) for w in flat),
        grid=(depth * n_e // per_step,),
        in_specs=[spec(w) for w in flat],
        out_specs=tuple(spec(w) for w in flat),
        compiler_params=_cparams(("arbitrary",)),
        name="expert_weight_cast",
    )(*flat)
    return tuple(o.reshape(w.shape) for o, w in zip(outs, (w_gate, w_up, w_down)))


def _expert_kernel(lo_ref, hi_ref, nu_ref, xs_ref, wg0_ref, wu0_ref, wd0_ref, wg1_ref, wu1_ref, wd1_ref, ys_ref):
    del lo_ref, hi_ref
    used = pl.program_id(0) < nu_ref[0]

    @pl.when(used)
    def _():
        d = ys_ref.shape[1]
        xb = xs_ref[:, 0:d].astype(BF16)
        gates = xs_ref[:, d:d + LANES]
        ys_ref[...] = (gates[:, 0:1] * _expert_mlp(xb, wg0_ref, wu0_ref, wd0_ref)
                       + gates[:, 1:2] * _expert_mlp(xb, wg1_ref, wu1_ref, wd1_ref))

    @pl.when(jnp.logical_not(used))
    def _():
        ys_ref[...] = jnp.zeros(ys_ref.shape, F32)


def _experts(blk_lo, blk_hi, n_used, xs, w_gate, w_up, w_down, layer):
    p_rows, dx = xs.shape
    d = dx - LANES
    bm = MOE_ROWS
    wspec = lambda shape, which: pl.BlockSpec(
        (None, None) + shape, lambda i, lo, hi, nu: (layer, (lo, hi)[which][i], 0, 0))
    grid_spec = pltpu.PrefetchScalarGridSpec(
        num_scalar_prefetch=3,
        grid=(p_rows // bm,),
        in_specs=[pl.BlockSpec((bm, dx), lambda i, lo, hi, nu: (i, 0)),
                  wspec((d, D_EXPERT), 0), wspec((d, D_EXPERT), 0), wspec((D_EXPERT, d), 0),
                  wspec((d, D_EXPERT), 1), wspec((d, D_EXPERT), 1), wspec((D_EXPERT, d), 1)],
        out_specs=pl.BlockSpec((bm, d), lambda i, lo, hi, nu: (i, 0)),
    )
    return pl.pallas_call(
        _expert_kernel,
        out_shape=jax.ShapeDtypeStruct((p_rows, d), F32),
        grid_spec=grid_spec,
        compiler_params=_cparams(("arbitrary",)),
        name="moe_experts",
    )(blk_lo, blk_hi, n_used, xs, w_gate, w_up, w_down, w_gate, w_up, w_down)


def _combine_kernel(ps_ref, rt_ref, rt_next_ref, x_ref, gt_ref, fg_ref, ys_ref, o_ref, ybuf, sems, *, final):
    i = pl.program_id(0)
    n_tok = x_ref.shape[0]
    slot = lax.rem(i, 2)

    def gather(rt, dst_slot):
        def issue(g, carry):
            for uu in range(DMA_UNROLL):
                r = g * DMA_UNROLL + uu
                _row_copy(ys_ref, _dest_row(ps_ref, rt, r), ybuf.at[dst_slot], r, sems.at[dst_slot]).start()
            return carry

        lax.fori_loop(0, n_tok // DMA_UNROLL, issue, 0)

    @pl.when(i == 0)
    def _():
        gather(rt_ref, 0)

    @pl.when(i + 1 < pl.num_programs(0))
    def _():
        gather(rt_next_ref, 1 - slot)

    pltpu.make_async_copy(ys_ref.at[pl.ds(0, n_tok), :], ybuf.at[slot], sems.at[slot]).wait()
    x = x_ref[...] + gt_ref[0] * ybuf[slot]
    if final:
        x = x * lax.rsqrt(jnp.mean(x * x, -1, keepdims=True) + NORM_EPS) * fg_ref[...]
    o_ref[...] = x


def _combine(pad_start, rt3, x, gt, final_g, ys, seq, final):
    t, d = x.shape
    tc = MOE_TOK
    per_b = seq // tc
    n_steps = t // tc
    grid_spec = pltpu.PrefetchScalarGridSpec(
        num_scalar_prefetch=1,
        grid=(n_steps,),
        in_specs=[pl.BlockSpec((1, 1, 2 * tc), lambda i, ps: (i, 0, 0), memory_space=pltpu.SMEM),
                  pl.BlockSpec((1, 1, 2 * tc), lambda i, ps: (jnp.minimum(i + 1, n_steps - 1), 0, 0),
                               memory_space=pltpu.SMEM),
                  pl.BlockSpec((tc, d), lambda i, ps: (i, 0)),
                  pl.BlockSpec((1, 1, d), lambda i, ps: (i // per_b, 0, 0)),
                  pl.BlockSpec((1, d), lambda i, ps: (0, 0)),
                  pl.BlockSpec(memory_space=pl.ANY)],
        out_specs=pl.BlockSpec((tc, d), lambda i, ps: (i, 0)),
        scratch_shapes=[pltpu.VMEM((2, tc, d), F32), pltpu.SemaphoreType.DMA((2,))],
    )
    return pl.pallas_call(
        functools.partial(_combine_kernel, final=final),
        out_shape=jax.ShapeDtypeStruct((t, d), F32),
        grid_spec=grid_spec,
        compiler_params=_cparams(("arbitrary",)),
        name="moe_combine",
    )(pad_start, rt3, rt3, x, gt, final_g, ys)


def _bucket_experts():
    lo, hi = [], []
    for g in range(N_GROUPS):
        for a in range(EXPERTS_PER_GROUP):
            for b in range(a + 1, EXPERTS_PER_GROUP):
                lo.append(g * EXPERTS_PER_GROUP + a)
                hi.append(g * EXPERTS_PER_GROUP + b)
    return jnp.asarray(lo, jnp.int32), jnp.asarray(hi, jnp.int32)


def _route_plan(route, counts, t):
    bm = MOE_ROWS
    cnt = counts[0, :N_BUCKETS].astype(jnp.int32)
    padded = (cnt + bm - 1) // bm * bm
    pad_end = jnp.cumsum(padded)
    pad_start = pad_end - padded
    p_rows = t + N_BUCKETS * bm
    n_blk = p_rows // bm
    blk_start = jnp.arange(n_blk, dtype=jnp.int32) * bm
    blk_b = jnp.minimum(jnp.sum((pad_end[None, :] <= blk_start[:, None]).astype(jnp.int32), axis=1), N_BUCKETS - 1)
    lo_tab, hi_tab = _bucket_experts()
    n_used = (pad_end[-1:] // bm).astype(jnp.int32)
    rt3 = route[:, 0:2].astype(jnp.int32).reshape(t // MOE_TOK, 1, 2 * MOE_TOK)
    return pad_start, rt3, lo_tab[blk_b], hi_tab[blk_b], n_used, p_rows


def kernel(x, c, ada_w, ada_b, mix_norm_g, ffn_norm_g, w_in, w_out, gdn_conv_w, gdn_a_log, gdn_dt_bias, gdn_norm_g,
           sgu_ln_g, sgu_ln_b, sgu_w, sgu_b, sc_conv_w, rw_mu, rw_w0, rw_w_up, rw_a0, rw_a_up, rw_g_up, rw_k_k,
           rw_k_a, rw_r_k, rw_gn_g, rw_gn_b, moe_w_group, moe_b_group, moe_w_router, moe_b_router, moe_w_gate,
           moe_w_up, moe_w_down, final_norm_g):
    bn, seq, d = x.shape
    depth = ada_w.shape[0]
    t = bn * seq
    assert d == 4 * GROUP_W and seq % ROW_BLOCK == 0 and ROW_BLOCK % MOE_TOK == 0
    g_w = GROUP_W
    mod = _ada(c, ada_w, ada_b)
    wg_b, wu_b, wd_b = _cast_expert_weights(moe_w_gate, moe_w_up, moe_w_down)
    xf = x.reshape(t, d)
    o_z, o_a, o_su = 3 * g_w, 4 * g_w, 4 * g_w + 2 * N_HEADS
    o_rp = o_su + 5 * g_w
    for l in range(depth):
        m = mod[l].reshape(bn, 6, 1, d)
        sh_m, sc_m, gt_m, sh_f, sc_f, gt_f = (m[:, i] for i in range(6))
        wl = w_in[l]
        w_ab = wl[:, o_a:o_su]
        w_r = jnp.concatenate([wl[:, 0:o_a], wl[:, o_su:o_rp], wl[:, o_rp:],
                               jnp.pad(w_ab, ((0, 0), (0, LANES - 2 * N_HEADS)))], axis=1).astype(BF16)
        pg, psc, prw, pab, abt = _in_proj(xf, mix_norm_g[l][None, :], sh_m, sc_m, w_r, w_ab.T.astype(BF16), seq)
        oa = _gdn(pg, pab, abt, gdn_conv_w[l], gdn_a_log[l], gdn_dt_bias[l], gdn_norm_g[l], bn, seq)
        obc = _sgu_conv(psc, sgu_ln_g[l], sgu_ln_b[l], sgu_w[l], sgu_b[l], sc_conv_w[l], bn, seq)
        od = _rwkv(prw, rw_mu[l], rw_w0[l], rw_w_up[l], rw_a0[l], rw_a_up[l], rw_g_up[l], rw_k_k[l], rw_k_a[l],
                   rw_r_k[l], rw_gn_g[l], rw_gn_b[l], bn, seq)
        w_rt = jnp.concatenate([moe_w_router[l], moe_w_group[l],
                                jnp.zeros((d, LANES - N_EXPERTS - N_GROUPS), F32)], axis=1)
        wr_hi = w_rt.astype(BF16)
        wr_lo = (w_rt - wr_hi.astype(F32)).astype(BF16)
        b_r = jnp.concatenate([moe_b_router[l], moe_b_group[l], jnp.zeros((LANES - N_EXPERTS - N_GROUPS,), F32)])[None, :]
        xf, hf, route, counts = _out_router(xf, oa, obc, od, w_out[l].astype(BF16), gt_m, ffn_norm_g[l][None, :],
                                            sh_f, sc_f, wr_hi, wr_lo, b_r, seq)
        pad_start, rt3, blk_lo, blk_hi, n_used, p_rows = _route_plan(route, counts, t)
        xs = _dispatch(pad_start, rt3, hf, p_rows)
        ys = _experts(blk_lo, blk_hi, n_used, xs, wg_b, wu_b, wd_b, l)
        xf = _combine(pad_start, rt3, xf, gt_f, final_norm_g[None, :], ys, seq, final=(l == depth - 1))
    return xf.reshape(bn, seq, d)
```

```python
import functools
import math

import jax
import jax.numpy as jnp
from jax import lax
from jax.experimental import pallas as pl
from jax.experimental.pallas import tpu as pltpu

F32 = jnp.float32
BF16 = jnp.bfloat16

HEAD_DIM = 64
N_HEADS = 4
GROUP_W = HEAD_DIM * N_HEADS
CHUNK = 64
TILE = 2 * CHUNK
NORM_EPS = 1e-6
SGU_CHUNK = 128
SGU_LN_EPS = 1e-5
RW_GN_EPS = 64e-5
RW_LORA_W, RW_LORA_A, RW_LORA_G = 64, 64, 128
N_GROUPS = 8
EXPERTS_PER_GROUP = 8
N_EXPERTS = N_GROUPS * EXPERTS_PER_GROUP
D_EXPERT = 256
PAIRS_PER_GROUP = EXPERTS_PER_GROUP * (EXPERTS_PER_GROUP - 1) // 2
N_BUCKETS = N_GROUPS * PAIRS_PER_GROUP
LANES = 128
VMEM_LIMIT = 56 * 1024 * 1024

ROW_BLOCK = 512
MOE_ROWS = 256
MOE_TOK = 512


def _dot(a, b):
    return jnp.dot(a.astype(BF16), b.astype(BF16), preferred_element_type=F32)


def _dot_nt(a, b):
    return lax.dot_general(a.astype(BF16), b.astype(BF16), (((1,), (1,)), ((), ())),
                           preferred_element_type=F32)


def _dot_tn(a, b):
    return lax.dot_general(a.astype(BF16), b.astype(BF16), (((0,), (0,)), ((), ())),
                           preferred_element_type=F32)


def _split2(x):
    hi = x.astype(BF16)
    lo = (x - hi.astype(F32)).astype(BF16)
    return hi, lo


def _dot_x_exact(x, m):
    hi, lo = _split2(x)
    return (jnp.dot(hi, m, preferred_element_type=F32) + jnp.dot(lo, m, preferred_element_type=F32))


def _dot_exact_x(m, x):
    hi, lo = _split2(x)
    return (jnp.dot(m, hi, preferred_element_type=F32) + jnp.dot(m, lo, preferred_element_type=F32))


def _sigmoid(x):
    return 1.0 / (1.0 + jnp.exp(-x))


def _silu(x):
    return x * _sigmoid(x)


def _softplus(x):
    return jnp.maximum(x, 0.0) + jnp.log(1.0 + jnp.exp(-jnp.abs(x)))


def _head_masks():
    lane = lax.broadcasted_iota(jnp.int32, (1, GROUP_W), 1)
    return [((lane >> 6) == h).astype(F32) for h in range(N_HEADS)]


def _tile_masks():
    ri = lax.broadcasted_iota(jnp.int32, (TILE, TILE), 0)
    ci = lax.broadcasted_iota(jnp.int32, (TILE, TILE), 1)
    same = (ri >> 6) == (ci >> 6)
    return same & (ri > ci), same & (ri >= ci), (ri == ci).astype(F32)


def _block_diag_mask():
    ri = lax.broadcasted_iota(jnp.int32, (GROUP_W, GROUP_W), 0)
    ci = lax.broadcasted_iota(jnp.int32, (GROUP_W, GROUP_W), 1)
    return ((ri >> 6) == (ci >> 6)).astype(F32)


def _merge_masks():
    ri = lax.broadcasted_iota(jnp.int32, (TILE, TILE), 0)
    ci = lax.broadcasted_iota(jnp.int32, (TILE, TILE), 1)
    return [((ri >> (l + 1)) == (ci >> (l + 1))) & (((ri >> l) & 1) == 1) & (((ci >> l) & 1) == 0) for l in range(6)]


def _unit_lower_inverses(a, eye, chains):
    masks = _merge_masks()
    d = {c: eye - jnp.where(masks[0], a[c], 0.0) for c in chains}
    for l in range(1, 6):
        f = {c: _dot(jnp.where(masks[l], a[c], 0.0), d[c]) for c in chains}
        d = {c: d[c] - _dot(d[c], f[c]) for c in chains}
    return d


def _seg_sum(x, seg):
    return jnp.dot(x.astype(BF16), seg, preferred_element_type=F32)


def _cparams(sem):
    return pltpu.CompilerParams(dimension_semantics=sem, vmem_limit_bytes=VMEM_LIMIT)


def _ada_kernel(c_ref, w_ref, b_ref, o_ref):
    c = c_ref[...]
    ca = _silu(c)
    chi, clo = _split2(ca)
    w = w_ref[...]
    whi, wlo = _split2(w)
    acc = jnp.dot(chi, whi, preferred_element_type=F32)
    acc += jnp.dot(clo, whi, preferred_element_type=F32)
    acc += jnp.dot(chi, wlo, preferred_element_type=F32)
    o_ref[...] = acc + b_ref[...]


def _ada(c, ada_w, ada_b):
    depth, d, d6 = ada_w.shape
    bn = c.shape[0]
    nj = d6 // d
    return pl.pallas_call(
        _ada_kernel,
        out_shape=jax.ShapeDtypeStruct((depth, bn, d6), F32),
        grid=(depth, nj),
        in_specs=[pl.BlockSpec((bn, d), lambda l, j: (0, 0)),
                  pl.BlockSpec((None, d, d), lambda l, j: (l, 0, j)),
                  pl.BlockSpec((None, 1, d), lambda l, j: (l, 0, j))],
        out_specs=pl.BlockSpec((None, bn, d), lambda l, j: (l, 0, j)),
        compiler_params=_cparams(("arbitrary", "arbitrary")),
        name="ada_mod",
    )(c, ada_w, ada_b.reshape(depth, 1, d6))


def _in_proj_kernel(x_ref, g_ref, sh_ref, sc_ref, w_ref, wab_ref, pg_ref, psc_ref, prw_ref, pab_ref, abt_ref):
    x = x_ref[...]
    y = x * lax.rsqrt(jnp.mean(x * x, -1, keepdims=True) + NORM_EPS) * g_ref[...]
    h = (y * (1.0 + sc_ref[0]) + sh_ref[0]).astype(BF16)
    o = 0
    for ref in (pg_ref, psc_ref, prw_ref, pab_ref):
        w = ref.shape[1]
        ref[...] = jnp.dot(h, w_ref[:, o:o + w], preferred_element_type=F32)
        o += w
    abt_ref[...] = lax.dot_general(wab_ref[...], h, (((1,), (1,)), ((), ())), preferred_element_type=F32)


def _in_proj(x, g, shift, scale, w_r, w_abt, seq):
    t, d = x.shape
    tm = ROW_BLOCK
    per_b = seq // tm
    widths = (4 * GROUP_W, 5 * GROUP_W, 4 * GROUP_W, LANES)
    bspec = pl.BlockSpec((1, 1, d), lambda i: (i // per_b, 0, 0))
    return pl.pallas_call(
        _in_proj_kernel,
        out_shape=tuple(jax.ShapeDtypeStruct((t, w), F32) for w in widths) + (jax.ShapeDtypeStruct((8, t), F32),),
        grid=(t // tm,),
        in_specs=[pl.BlockSpec((tm, d), lambda i: (i, 0)),
                  pl.BlockSpec((1, d), lambda i: (0, 0)),
                  bspec, bspec,
                  pl.BlockSpec(w_r.shape, lambda i: (0, 0)),
                  pl.BlockSpec(w_abt.shape, lambda i: (0, 0))],
        out_specs=tuple(pl.BlockSpec((tm, w), lambda i: (i, 0)) for w in widths) + (pl.BlockSpec((8, tm), lambda i: (0, i)),),
        compiler_params=_cparams(("arbitrary",)),
        name="in_proj",
    )(x, g, shift, scale, w_r, w_abt)


def _gdn_kernel(p_ref, ab_ref, abt_ref, cw_ref, alog_ref, dtb_ref, alogt_ref, dtbt_ref, ng_ref,
                eg_ref, eb_ref, seg_ref, tri_ref, trit_ref, full_ref,
                o_ref,
                xbuf, s_ref, q_s, k_s, kb_s, rhs_s, qd_s, kt_s, cd_s, gc4_s, gct_s, u_s, w_s, snap_s):
    rows = p_ref.shape[0]
    j = pl.program_id(1)

    @pl.when(j == 0)
    def _():
        xbuf[0:8, :] = jnp.zeros((8, xbuf.shape[1]), F32)
        s_ref[...] = jnp.zeros(s_ref.shape, F32)

    xbuf[8:8 + rows, :] = p_ref[:, 0:3 * GROUP_W]
    acc = cw_ref[3:4, :] * xbuf[8:8 + rows, :]
    for tap in range(3):
        acc = acc + cw_ref[tap:tap + 1, :] * xbuf[5 + tap:5 + tap + rows, :]
    xbuf[0:8, :] = xbuf[rows:rows + 8, :]
    qkv = _silu(acc)
    seg = seg_ref[...]
    q = qkv[:, 0:GROUP_W]
    k = qkv[:, GROUP_W:2 * GROUP_W]
    v = qkv[:, 2 * GROUP_W:3 * GROUP_W]
    q = q * lax.rsqrt(_seg_sum(q * q, seg) + 1e-6) * (HEAD_DIM ** -0.5)
    k = k * lax.rsqrt(_seg_sum(k * k, seg) + 1e-6)

    ab = ab_ref[...]
    g = -jnp.exp(alog_ref[...]) * _softplus(ab + dtb_ref[...])
    beta = jnp.dot(_sigmoid(ab).astype(BF16), eb_ref[...], preferred_element_type=F32)
    tsl = [slice(t * TILE, (t + 1) * TILE) for t in range(rows // TILE)]
    gc4 = jnp.concatenate([_dot_exact_x(tri_ref[...], g[sl]) for sl in tsl], axis=0)
    gl4 = jnp.concatenate([_dot_exact_x(full_ref[...], g[sl]) for sl in tsl], axis=0)
    gc4_s[...] = gc4
    gc = _dot_x_exact(gc4, eg_ref[...])
    gl = _dot_x_exact(gl4, eg_ref[...])
    egc = jnp.exp(gc)
    kb = k * beta
    q_s[...] = q
    k_s[...] = k
    kb_s[...] = kb
    rhs_s[:, 0:GROUP_W] = v * beta
    rhs_s[:, GROUP_W:2 * GROUP_W] = kb * egc
    qd_s[...] = q * egc
    kt_s[...] = k * jnp.exp(gl - gc)
    cd_s[...] = jnp.exp(gl)
    abt = abt_ref[...]
    gt = -jnp.exp(alogt_ref[...]) * _softplus(abt + dtbt_ref[...])
    for t in range(rows // TILE):
        gct_s[t] = _dot_x_exact(gt[:, tsl[t]], trit_ref[...])

    hm = _head_masks()
    strict, incl, eye = _tile_masks()
    bd = _block_diag_mask()
    ng = ng_ref[...]
    tiles = range(rows // TILE)
    heads = range(N_HEADS)
    chains = [(t, h) for t in tiles for h in heads]
    rsl = [slice(t * TILE, (t + 1) * TILE) for t in tiles]

    x, attn = {}, {}
    for t in tiles:
        kt_ = k_s[rsl[t], :].astype(BF16)
        kbt, qt = kb_s[rsl[t], :], q_s[rsl[t], :]
        gct_t = gct_s[t]
        gc4_t = gc4_s[rsl[t], :]
        for h in heads:
            dec = jnp.exp(jnp.where(incl, gc4_t[:, h:h + 1] - gct_t[h:h + 1, :], -jnp.inf))
            x[t, h] = jnp.where(strict, _dot_nt(kbt * hm[h], kt_) * dec, 0.0)
            attn[t, h] = (_dot_nt(qt * hm[h], kt_) * dec).astype(BF16)
    p = _unit_lower_inverses(x, eye, chains)
    for t in tiles:
        rhs = rhs_s[rsl[t], :].astype(BF16)
        u = jnp.zeros((TILE, GROUP_W), F32)
        w = jnp.zeros((TILE, GROUP_W), F32)
        for h in heads:
            sol = _dot(p[t, h], rhs)
            u = u + hm[h] * sol[:, 0:GROUP_W]
            w = w + hm[h] * sol[:, GROUP_W:2 * GROUP_W]
        u_s[rsl[t], :] = u
        w_s[rsl[t], :] = w

    chunks = range(rows // CHUNK)
    csl = [slice(c * CHUNK, (c + 1) * CHUNK) for c in chunks]
    pq = [_dot_tn(kt_s[csl[c], :], jnp.concatenate([w_s[csl[c], :], u_s[csl[c], :]], axis=1)) for c in chunks]
    pmat = [(bd * pq[c][:, 0:GROUP_W]).astype(BF16) for c in chunks]
    s = s_ref[...]
    for c in chunks:
        snap_s[c] = s.astype(BF16)
        s = s * cd_s[c * CHUNK:c * CHUNK + 1, :] - _dot(pmat[c], s) + bd * pq[c][:, GROUP_W:2 * GROUP_W]
    s_ref[...] = s
    for c in chunks:
        ws = jnp.dot(jnp.concatenate([w_s[csl[c], :], qd_s[csl[c], :]], axis=0).astype(BF16), snap_s[c],
                     preferred_element_type=F32)
        u_s[csl[c], :] = u_s[csl[c], :] - ws[0:CHUNK]
        w_s[csl[c], :] = ws[CHUNK:2 * CHUNK]

    for t in tiles:
        vn = u_s[rsl[t], :].astype(BF16)
        o = w_s[rsl[t], :]
        for h in heads:
            o = o + hm[h] * _dot(attn[t, h], vn)
        o = o * lax.rsqrt(_seg_sum(o * o, seg) * (1.0 / HEAD_DIM) + NORM_EPS) * ng
        o_ref[rsl[t], :] = (o * _silu(p_ref[rsl[t], 3 * GROUP_W:4 * GROUP_W])).astype(o_ref.dtype)


def _chunk_mats():
    ri = jnp.arange(TILE)[:, None]
    ci = jnp.arange(TILE)[None, :]
    same = (ri // CHUNK) == (ci // CHUNK)
    tri = (same & (ci <= ri)).astype(BF16)
    return tri, tri.T, same.astype(BF16)


def _expand_mats():
    lane = jnp.arange(LANES)[:, None]
    col = jnp.arange(GROUP_W)[None, :]
    eg = (lane == col // HEAD_DIM).astype(BF16)
    eb = (lane == N_HEADS + col // HEAD_DIM).astype(BF16)
    seg = ((jnp.arange(GROUP_W)[:, None] // HEAD_DIM) == (col // HEAD_DIM)).astype(BF16)
    return eg, eb, seg


def _pad_lanes(v, n=LANES):
    return jnp.zeros((1, n), F32).at[0, :v.shape[0]].set(v.astype(F32))


def _gdn(pg, pab, abt, conv_w, a_log, dt_bias, norm_g, bn, seq):
    t = pg.shape[0]
    rows = ROW_BLOCK
    nb = seq // rows
    eg, eb, seg = _expand_mats()
    tri, trit, full = _chunk_mats()
    alog_t = jnp.zeros((8, rows), F32).at[:N_HEADS].set(jnp.broadcast_to(a_log[:, None], (N_HEADS, rows)))
    dtb_t = jnp.zeros((8, rows), F32).at[:N_HEADS].set(jnp.broadcast_to(dt_bias[:, None], (N_HEADS, rows)))
    consts = (conv_w.astype(F32), _pad_lanes(a_log), _pad_lanes(dt_bias), alog_t, dtb_t,
              jnp.tile(norm_g.astype(F32), N_HEADS)[None, :], eg, eb, seg, tri, trit, full)
    rowmap = lambda b, j: (b * nb + j, 0)
    return pl.pallas_call(
        _gdn_kernel,
        out_shape=jax.ShapeDtypeStruct((t, GROUP_W), BF16),
        grid=(bn, nb),
        in_specs=[pl.BlockSpec((rows, 4 * GROUP_W), rowmap),
                  pl.BlockSpec((rows, LANES), rowmap),
                  pl.BlockSpec((8, rows), lambda b, j: (0, b * nb + j))]
                 + [pl.BlockSpec(c.shape, lambda b, j: (0, 0)) for c in consts],
        out_specs=pl.BlockSpec((rows, GROUP_W), rowmap),
        scratch_shapes=[pltpu.VMEM((rows + 8, 3 * GROUP_W), F32),
                        pltpu.VMEM((GROUP_W, GROUP_W), F32),
                        pltpu.VMEM((rows, GROUP_W), F32), pltpu.VMEM((rows, GROUP_W), F32),
                        pltpu.VMEM((rows, GROUP_W), F32), pltpu.VMEM((rows, 2 * GROUP_W), F32),
                        pltpu.VMEM((rows, GROUP_W), F32), pltpu.VMEM((rows, GROUP_W), F32),
                        pltpu.VMEM((rows, GROUP_W), F32), pltpu.VMEM((rows, LANES), F32),
                        pltpu.VMEM((rows // TILE, 8, TILE), F32),
                        pltpu.VMEM((rows, GROUP_W), F32), pltpu.VMEM((rows, GROUP_W), F32),
                        pltpu.VMEM((rows // CHUNK, GROUP_W, GROUP_W), BF16)],
        compiler_params=_cparams(("arbitrary", "arbitrary")),
        name="gdn_mixer",
    )(pg, pab, abt, *consts)


def _sgu_conv_kernel(p_ref, lng_ref, lnb_ref, ws_ref, bs_ref, cw_ref, o_ref, xbuf):
    rows = p_ref.shape[0]
    j = pl.program_id(1)

    @pl.when(j == 0)
    def _():
        xbuf[0:8, :] = jnp.zeros((8, GROUP_W), F32)

    u = jax.nn.gelu(p_ref[:, 0:GROUP_W])
    vf = jax.nn.gelu(p_ref[:, GROUP_W:2 * GROUP_W])
    mean = jnp.mean(vf, -1, keepdims=True)
    var = jnp.mean(jnp.square(vf - mean), -1, keepdims=True)
    v = (vf - mean) * lax.rsqrt(var + SGU_LN_EPS) * lng_ref[...] + lnb_ref[...]
    hm = _head_masks()
    ri = lax.broadcasted_iota(jnp.int32, (SGU_CHUNK, SGU_CHUNK), 0)
    ci = lax.broadcasted_iota(jnp.int32, (SGU_CHUNK, SGU_CHUNK), 1)
    ws = [jnp.where(ri >= ci, ws_ref[h], 0.0).astype(BF16) for h in range(N_HEADS)]
    bs = bs_ref[...]
    for c in range(rows // SGU_CHUNK):
        cs = slice(c * SGU_CHUNK, (c + 1) * SGU_CHUNK)
        vc = v[cs].astype(BF16)
        mixed = bs
        for h in range(N_HEADS):
            mixed = mixed + hm[h] * jnp.dot(ws[h], vc, preferred_element_type=F32)
        o_ref[cs, 0:GROUP_W] = (u[cs] * mixed).astype(o_ref.dtype)

    xbuf[8:8 + rows, :] = p_ref[:, 3 * GROUP_W:4 * GROUP_W] * p_ref[:, 4 * GROUP_W:5 * GROUP_W]
    acc = cw_ref[2:3, :] * xbuf[8:8 + rows, :]
    for tap in range(2):
        acc = acc + cw_ref[tap:tap + 1, :] * xbuf[6 + tap:6 + tap + rows, :]
    xbuf[0:8, :] = xbuf[rows:rows + 8, :]
    o_ref[:, GROUP_W:2 * GROUP_W] = (p_ref[:, 2 * GROUP_W:3 * GROUP_W] * acc).astype(o_ref.dtype)


def _sgu_conv(psc, ln_g, ln_b, w_s, b_s, conv_w, bn, seq):
    t = psc.shape[0]
    rows = ROW_BLOCK
    nb = seq // rows
    bs_exp = jnp.repeat(b_s.T.astype(F32), HEAD_DIM, axis=1)
    consts = (ln_g[None, :].astype(F32), ln_b[None, :].astype(F32), w_s.astype(F32), bs_exp, conv_w.astype(F32))
    rowmap = lambda b, j: (b * nb + j, 0)
    return pl.pallas_call(
        _sgu_conv_kernel,
        out_shape=jax.ShapeDtypeStruct((t, 2 * GROUP_W), BF16),
        grid=(bn, nb),
        in_specs=[pl.BlockSpec((rows, 5 * GROUP_W), rowmap)]
                 + [pl.BlockSpec(c.shape, lambda b, j, n=c.ndim: (0,) * n) for c in consts],
        out_specs=pl.BlockSpec((rows, 2 * GROUP_W), rowmap),
        scratch_shapes=[pltpu.VMEM((rows + 8, GROUP_W), F32)],
        compiler_params=_cparams(("arbitrary", "arbitrary")),
        name="sgu_conv_mixer",
    )(psc, *consts)


def _rwkv_kernel(p_ref, mu_ref, w0_ref, wup_ref, a0_ref, aup_ref, gup_ref, kk_ref, ka_ref, rk_ref, gng_ref, gnb_ref,
                 seg_ref, tri_ref, full_ref,
                 o_ref,
                 prev, s_ref, at_s, bt_s, kt_s, rt_s, v_s, btl_s, ktl_s, gam_s, bon_s, gate_s, wa_s, u_s, y_s, snap_s):
    rows = p_ref.shape[0]
    j = pl.program_id(1)

    @pl.when(j == 0)
    def _():
        prev[...] = jnp.zeros(prev.shape, F32)
        s_ref[...] = jnp.zeros(s_ref.shape, F32)

    prev[8:8 + rows, :] = p_ref[...]
    p = p_ref[...]
    p = p + (prev[7:7 + rows, :] - p) * mu_ref[...]
    prev[0:8, :] = prev[rows:rows + 8, :]
    g_w = GROUP_W
    r = p[:, 0:g_w]
    k = p[:, g_w:2 * g_w]
    v = p[:, 2 * g_w:3 * g_w]
    o = 3 * g_w
    xw = p[:, o:o + RW_LORA_W]
    xa = p[:, o + RW_LORA_W:o + RW_LORA_W + RW_LORA_A]
    xg = p[:, o + RW_LORA_W + RW_LORA_A:o + RW_LORA_W + RW_LORA_A + RW_LORA_G]
    w_log = -_softplus(-(w0_ref[...] + _dot(jnp.tanh(xw), wup_ref[...]))) - 0.5
    lw = -jnp.exp(w_log)
    a = _sigmoid(a0_ref[...] + _dot(xa, aup_ref[...]))
    gate_s[...] = _dot(_sigmoid(xg), gup_ref[...])
    seg = seg_ref[...]
    kk = k * kk_ref[...]
    kk = kk * lax.rsqrt(_seg_sum(kk * kk, seg) + 1e-12)
    k_mod = k * (1.0 + (a - 1.0) * ka_ref[...])
    bon_s[...] = _seg_sum(r * k_mod * rk_ref[...], seg) * v
    tsl = [slice(t * TILE, (t + 1) * TILE) for t in range(rows // TILE)]
    cl = jnp.concatenate([_dot_exact_x(tri_ref[...], lw[sl]) for sl in tsl], axis=0)
    ct = jnp.concatenate([_dot_exact_x(full_ref[...], lw[sl]) for sl in tsl], axis=0)
    e_neg = jnp.exp(-cl)
    e_tail = jnp.exp(ct - cl)
    zb = kk * a
    at_s[...] = -kk * jnp.exp(cl - lw)
    bt_s[...] = zb * e_neg
    kt_s[...] = k_mod * e_neg
    rt_s[...] = r * jnp.exp(cl)
    v_s[...] = v
    btl_s[...] = zb * e_tail
    ktl_s[...] = k_mod * e_tail
    gam_s[...] = jnp.exp(ct)

    hm = _head_masks()
    strict, incl, eye = _tile_masks()
    bd = _block_diag_mask()
    gng, gnb = gng_ref[...], gnb_ref[...]

    tiles = range(rows // TILE)
    heads = range(N_HEADS)
    chains = [(t, h) for t in tiles for h in heads]
    rsl = [slice(t * TILE, (t + 1) * TILE) for t in tiles]

    x, lak, mrb, mrk = {}, {}, {}, {}
    for t in tiles:
        at, rt = at_s[rsl[t], :], rt_s[rsl[t], :]
        rhs_nt = jnp.concatenate([bt_s[rsl[t], :], kt_s[rsl[t], :]], axis=0).astype(BF16)
        for h in heads:
            sc = _dot_nt(jnp.concatenate([at * hm[h], rt * hm[h]], axis=0), rhs_nt)
            x[t, h] = jnp.where(strict, -sc[0:TILE, 0:TILE], 0.0)
            lak[t, h] = jnp.where(strict, sc[0:TILE, TILE:2 * TILE], 0.0).astype(BF16)
            mrb[t, h] = jnp.where(incl, sc[TILE:2 * TILE, 0:TILE], 0.0).astype(BF16)
            mrk[t, h] = jnp.where(incl, sc[TILE:2 * TILE, TILE:2 * TILE], 0.0).astype(BF16)
    p = _unit_lower_inverses(x, eye, chains)
    for t in tiles:
        at = at_s[rsl[t], :]
        vt = v_s[rsl[t], :].astype(BF16)
        wa = jnp.zeros((TILE, GROUP_W), F32)
        u0 = jnp.zeros((TILE, GROUP_W), F32)
        y0 = jnp.zeros((TILE, GROUP_W), F32)
        for h in heads:
            sol = _dot(p[t, h], jnp.concatenate([at, _dot(lak[t, h], vt)], axis=1))
            wa = wa + hm[h] * sol[:, 0:GROUP_W]
            u0 = u0 + hm[h] * sol[:, GROUP_W:2 * GROUP_W]
            y0 = y0 + hm[h] * _dot(mrk[t, h], vt)
        wa_s[rsl[t], :] = wa
        u_s[rsl[t], :] = u0
        y_s[rsl[t], :] = y0

    chunks = range(rows // CHUNK)
    csl = [slice(c * CHUNK, (c + 1) * CHUNK) for c in chunks]
    pmat = [(bd * _dot_tn(wa_s[csl[c], :], btl_s[csl[c], :])).astype(BF16) for c in chunks]
    qmat = [bd * _dot_tn(jnp.concatenate([u_s[csl[c], :], v_s[csl[c], :]], axis=0),
                         jnp.concatenate([btl_s[csl[c], :], ktl_s[csl[c], :]], axis=0)) for c in chunks]
    s = s_ref[...]
    for c in chunks:
        snap_s[c] = s.astype(BF16)
        s = s * gam_s[c * CHUNK:c * CHUNK + 1, :] + _dot(s, pmat[c]) + qmat[c]
    s_ref[...] = s
    for c in chunks:
        us = _dot_nt(jnp.concatenate([wa_s[csl[c], :], rt_s[csl[c], :]], axis=0), snap_s[c])
        u_s[csl[c], :] = u_s[csl[c], :] + us[0:CHUNK]
        y_s[csl[c], :] = y_s[csl[c], :] + us[CHUNK:2 * CHUNK]

    for t in tiles:
        u = u_s[rsl[t], :].astype(BF16)
        y = y_s[rsl[t], :]
        for h in heads:
            y = y + hm[h] * _dot(mrb[t, h], u)
        mean = _seg_sum(y, seg) * (1.0 / HEAD_DIM)
        yc = y - mean
        var = _seg_sum(yc * yc, seg) * (1.0 / HEAD_DIM)
        yn = yc * lax.rsqrt(var + RW_GN_EPS) * gng + gnb
        o_ref[rsl[t], :] = ((yn + bon_s[rsl[t], :]) * gate_s[rsl[t], :]).astype(o_ref.dtype)


def _rwkv(prw, mu, w0, w_up, a0, a_up, g_up, k_k, k_a, r_k, gn_g, gn_b, bn, seq):
    t = prw.shape[0]
    rows = ROW_BLOCK
    nb = seq // rows
    _, _, seg = _expand_mats()
    tri, _, full = _chunk_mats()
    row = lambda x: x.reshape(1, -1).astype(F32)
    consts = (row(mu), row(w0), w_up.astype(BF16), row(a0), a_up.astype(BF16), g_up.astype(BF16),
              row(k_k), row(k_a), row(r_k), row(gn_g), row(gn_b), seg, tri, full)
    rowmap = lambda b, j: (b * nb + j, 0)
    big = lambda: pltpu.VMEM((rows, GROUP_W), F32)
    return pl.pallas_call(
        _rwkv_kernel,
        out_shape=jax.ShapeDtypeStruct((t, GROUP_W), BF16),
        grid=(bn, nb),
        in_specs=[pl.BlockSpec((rows, 4 * GROUP_W), rowmap)]
                 + [pl.BlockSpec(c.shape, lambda b, j: (0, 0)) for c in consts],
        out_specs=pl.BlockSpec((rows, GROUP_W), rowmap),
        scratch_shapes=[pltpu.VMEM((rows + 8, 4 * GROUP_W), F32), pltpu.VMEM((GROUP_W, GROUP_W), F32)]
                       + [big() for _ in range(13)] + [pltpu.VMEM((rows // CHUNK, GROUP_W, GROUP_W), BF16)],
        compiler_params=_cparams(("arbitrary", "arbitrary")),
        name="rwkv7_mixer",
    )(prw, *consts)


def _out_router_kernel(x_ref, oa_ref, obc_ref, od_ref, wo_ref, gt_ref, g_ref, sh_ref, sc_ref,
                       wrh_ref, wrl_ref, br_ref, tri_ref,
                       xo_ref, hf_ref, route_ref, cnt_ref, carry):
    i = pl.program_id(0)

    @pl.when(i == 0)
    def _():
        carry[...] = jnp.zeros(carry.shape, F32)

    g_w = GROUP_W
    tm, d_model = x_ref.shape
    sub = TILE
    subs = [slice(s * sub, (s + 1) * sub) for s in range(tm // sub)]
    lane = lax.broadcasted_iota(jnp.int32, (sub, LANES), 1)
    lanef = lane.astype(F32)
    lane2 = lax.broadcasted_iota(jnp.int32, (sub, 2 * LANES), 1).astype(F32)
    big = jnp.float32(1e9)
    ninf = jnp.float32(-jnp.inf)
    is_g = (lane >= N_EXPERTS) & (lane < N_EXPERTS + N_GROUPS)

    def out_proj(rs):
        mixed = jnp.dot(oa_ref[rs, :], wo_ref[0:g_w, :], preferred_element_type=F32)
        mixed += jnp.dot(obc_ref[rs, :], wo_ref[g_w:3 * g_w, :], preferred_element_type=F32)
        mixed += jnp.dot(od_ref[rs, :], wo_ref[3 * g_w:4 * g_w, :], preferred_element_type=F32)
        return mixed

    def residual_norm(rs, mixed):
        x = x_ref[rs, :] + gt_ref[0] * mixed
        xo_ref[rs, :] = x
        y = x * lax.rsqrt(jnp.mean(x * x, -1, keepdims=True) + NORM_EPS) * g_ref[...]
        hf = y * (1.0 + sc_ref[0]) + sh_ref[0]
        hf_ref[rs, 0:d_model] = hf
        return _split2(hf)

    def logits(hh, hl):
        return (jnp.dot(hh, wrh_ref[...], preferred_element_type=F32) + jnp.dot(hl, wrh_ref[...], preferred_element_type=F32)
                + jnp.dot(hh, wrl_ref[...], preferred_element_type=F32) + br_ref[...])

    def route(rs, lg):
        gl = jnp.where(is_g, lg, ninf)
        gmax = jnp.max(gl, -1, keepdims=True)
        gsel = jnp.min(jnp.where(gl == gmax, lanef - N_EXPERTS, big), -1, keepdims=True)
        p_group = 1.0 / jnp.sum(jnp.where(is_g, jnp.exp(gl - gmax), 0.0), -1, keepdims=True)
        in_grp = (lane < N_EXPERTS) & ((lane >> 3).astype(F32) == gsel)
        el = jnp.where(in_grp, lg, ninf)
        v1 = jnp.max(el, -1, keepdims=True)
        i1 = jnp.min(jnp.where(el == v1, lanef, big), -1, keepdims=True)
        el2 = jnp.where(lanef == i1, ninf, el)
        v2 = jnp.max(el2, -1, keepdims=True)
        i2 = jnp.min(jnp.where(el2 == v2, lanef, big), -1, keepdims=True)
        e21 = jnp.exp(v2 - v1)
        g1 = p_group / (1.0 + e21)
        g2 = p_group * e21 / (1.0 + e21)
        first_lo = i1 < i2
        a = jnp.where(first_lo, i1, i2) - gsel * EXPERTS_PER_GROUP
        b = jnp.where(first_lo, i2, i1) - gsel * EXPERTS_PER_GROUP
        bucket = gsel * PAIRS_PER_GROUP + a * (2 * EXPERTS_PER_GROUP - 1 - a) * 0.5 + (b - a - 1.0)
        g_lo = jnp.where(first_lo, g1, g2)
        g_hi = jnp.where(first_lo, g2, g1)
        hf_ref[rs, d_model:d_model + LANES] = jnp.where(lane == 0, g_lo, jnp.where(lane == 1, g_hi, 0.0))
        return bucket, jnp.where(lane2 == bucket, 1.0, 0.0)

    mixed = [out_proj(rs) for rs in subs]
    halves = [residual_norm(rs, m) for rs, m in zip(subs, mixed)]
    lgs = [logits(hh, hl) for hh, hl in halves]
    routed = [route(rs, lg) for rs, lg in zip(subs, lgs)]
    oh = jnp.concatenate([r[1] for r in routed], axis=0)
    total = jnp.dot(tri_ref[...], oh.astype(BF16), preferred_element_type=F32) + carry[...]
    rank = jnp.sum(oh * total, -1, keepdims=True)
    carry[...] = carry[...] + jnp.sum(oh, axis=0, keepdims=True)
    cnt_ref[...] = carry[...]
    for s, rs in enumerate(subs):
        route_ref[rs, :] = jnp.where(lane == 0, routed[s][0], jnp.where(lane == 1, rank[rs], 0.0))


def _out_router(x, oa, obc, od, w_out, gt, g, shift, scale, wr_hi, wr_lo, b_r, seq):
    t, d = x.shape
    tm = ROW_BLOCK
    per_b = seq // tm
    tri = (jnp.arange(tm)[:, None] > jnp.arange(tm)[None, :]).astype(BF16)
    bspec = pl.BlockSpec((1, 1, d), lambda i: (i // per_b, 0, 0))
    full = lambda a: pl.BlockSpec(a.shape, lambda i: (0,) * a.ndim)
    return pl.pallas_call(
        _out_router_kernel,
        out_shape=(jax.ShapeDtypeStruct((t, d), F32), jax.ShapeDtypeStruct((t, d + LANES), F32),
                   jax.ShapeDtypeStruct((t, LANES), F32), jax.ShapeDtypeStruct((1, 2 * LANES), F32)),
        grid=(t // tm,),
        in_specs=[pl.BlockSpec((tm, d), lambda i: (i, 0)),
                  pl.BlockSpec((tm, GROUP_W), lambda i: (i, 0)),
                  pl.BlockSpec((tm, 2 * GROUP_W), lambda i: (i, 0)),
                  pl.BlockSpec((tm, GROUP_W), lambda i: (i, 0)),
                  full(w_out), bspec, full(g), bspec, bspec, full(wr_hi), full(wr_lo), full(b_r), full(tri)],
        out_specs=(pl.BlockSpec((tm, d), lambda i: (i, 0)), pl.BlockSpec((tm, d + LANES), lambda i: (i, 0)),
                   pl.BlockSpec((tm, LANES), lambda i: (i, 0)), pl.BlockSpec((1, 2 * LANES), lambda i: (0, 0))),
        scratch_shapes=[pltpu.VMEM((1, 2 * LANES), F32)],
        compiler_params=_cparams(("arbitrary",)),
        name="out_proj_router",
    )(x, oa, obc, od, w_out, gt, g, shift, scale, wr_hi, wr_lo, b_r, tri)


def _row_copy(src, src_row, dst, dst_row, sem):
    return pltpu.make_async_copy(src.at[pl.ds(src_row, 1), :], dst.at[pl.ds(dst_row, 1), :], sem)


DMA_UNROLL = 16


def _dest_row(ps_ref, rt_ref, r):
    return ps_ref[rt_ref[0, 0, 2 * r]] + rt_ref[0, 0, 2 * r + 1]


STAGE_SLOTS = 3


def _dispatch_kernel(ps_ref, rt_ref, hf_ref, xs_in_ref, xs_ref, stage, stage_sems, row_sems):
    del xs_in_ref
    i = pl.program_id(0)
    last = pl.num_programs(0) - 1
    n_tok = stage.shape[1]
    slot = lax.rem(i, STAGE_SLOTS)

    def stage_copy(blk, s):
        return pltpu.make_async_copy(hf_ref.at[pl.ds(blk * n_tok, n_tok), :], stage.at[s], stage_sems.at[s])

    def rows_done(s):
        pltpu.make_async_copy(stage.at[s], xs_ref.at[pl.ds(0, n_tok), :], row_sems.at[s]).wait()

    @pl.when(i == 0)
    def _():
        stage_copy(0, 0).start()

    @pl.when(i < last)
    def _():
        stage_copy(i + 1, lax.rem(i + 1, STAGE_SLOTS)).start()

    stage_copy(i, slot).wait()

    def issue(g, carry):
        for uu in range(DMA_UNROLL):
            r = g * DMA_UNROLL + uu
            _row_copy(stage.at[slot], r, xs_ref, _dest_row(ps_ref, rt_ref, r), row_sems.at[slot]).start()
        return carry

    lax.fori_loop(0, n_tok // DMA_UNROLL, issue, 0)

    @pl.when(i > 0)
    def _():
        rows_done(lax.rem(i + STAGE_SLOTS - 1, STAGE_SLOTS))

    @pl.when(i == last)
    def _():
        rows_done(slot)


def _dispatch(pad_start, rt3, hf, p_rows):
    t, d = hf.shape
    td = MOE_TOK
    xs0 = jnp.zeros((p_rows, d), F32)
    grid_spec = pltpu.PrefetchScalarGridSpec(
        num_scalar_prefetch=1,
        grid=(t // td,),
        in_specs=[pl.BlockSpec((1, 1, 2 * td), lambda i, ps: (i, 0, 0), memory_space=pltpu.SMEM),
                  pl.BlockSpec(memory_space=pl.ANY),
                  pl.BlockSpec(memory_space=pl.ANY)],
        out_specs=pl.BlockSpec(memory_space=pl.ANY),
        scratch_shapes=[pltpu.VMEM((STAGE_SLOTS, td, d), F32), pltpu.SemaphoreType.DMA((STAGE_SLOTS,)),
                        pltpu.SemaphoreType.DMA((STAGE_SLOTS,))],
    )
    return pl.pallas_call(
        _dispatch_kernel,
        out_shape=jax.ShapeDtypeStruct((p_rows, d), F32),
        grid_spec=grid_spec,
        input_output_aliases={3: 0},
        compiler_params=_cparams(("arbitrary",)),
        name="moe_dispatch",
    )(pad_start, rt3, hf, xs0)


def _expert_mlp(xb, wg_b, wu_b, wd_b):
    hid = _silu(jnp.dot(xb, wg_b[...], preferred_element_type=F32)) * jnp.dot(xb, wu_b[...], preferred_element_type=F32)
    return jnp.dot(hid.astype(BF16), wd_b[...], preferred_element_type=F32)


def _cast_kernel(a_ref, b_ref, c_ref, ao_ref, bo_ref, co_ref):
    ao_ref[...] = a_ref[...].astype(BF16)
    bo_ref[...] = b_ref[...].astype(BF16)
    co_ref[...] = c_ref[...].astype(BF16)


def _cast_expert_weights(w_gate, w_up, w_down):
    depth, n_e = w_gate.shape[:2]
    per_step = 2
    flat = [w.reshape((depth * n_e,) + w.shape[2:]) for w in (w_gate, w_up, w_down)]
    spec = lambda w: pl.BlockSpec((per_step,) + w.shape[1:], lambda i: (i, 0, 0))
    outs = pl.pallas_call(
        _cast_kernel,
        out_shape=tuple(jax.ShapeDtypeStruct(w.shape, BF16) for w in flat),
        grid=(depth * n_e // per_step,),
        in_specs=[spec(w) for w in flat],
        out_specs=tuple(spec(w) for w in flat),
        compiler_params=_cparams(("arbitrary",)),
        name="expert_weight_cast",
    )(*flat)
    return tuple(o.reshape(w.shape) for o, w in zip(outs, (w_gate, w_up, w_down)))


def _expert_kernel(lo_ref, hi_ref, nu_ref, xs_ref, wg0_ref, wu0_ref, wd0_ref, wg1_ref, wu1_ref, wd1_ref, ys_ref):
    del lo_ref, hi_ref
    used = pl.program_id(0) < nu_ref[0]

    @pl.when(used)
    def _():
        d = ys_ref.shape[1]
        xb = xs_ref[:, 0:d].astype(BF16)
        gates = xs_ref[:, d:d + LANES]
        ys_ref[...] = (gates[:, 0:1] * _expert_mlp(xb, wg0_ref, wu0_ref, wd0_ref)
                       + gates[:, 1:2] * _expert_mlp(xb, wg1_ref, wu1_ref, wd1_ref))

    @pl.when(jnp.logical_not(used))
    def _():
        ys_ref[...] = jnp.zeros(ys_ref.shape, F32)


def _experts(blk_lo, blk_hi, n_used, xs, w_gate, w_up, w_down, layer):
    p_rows, dx = xs.shape
    d = dx - LANES
    bm = MOE_ROWS
    wspec = lambda shape, which: pl.BlockSpec(
        (None, None) + shape, lambda i, lo, hi, nu: (layer, (lo, hi)[which][i], 0, 0))
    grid_spec = pltpu.PrefetchScalarGridSpec(
        num_scalar_prefetch=3,
        grid=(p_rows // bm,),
        in_specs=[pl.BlockSpec((bm, dx), lambda i, lo, hi, nu: (i, 0)),
                  wspec((d, D_EXPERT), 0), wspec((d, D_EXPERT), 0), wspec((D_EXPERT, d), 0),
                  wspec((d, D_EXPERT), 1), wspec((d, D_EXPERT), 1), wspec((D_EXPERT, d), 1)],
        out_specs=pl.BlockSpec((bm, d), lambda i, lo, hi, nu: (i, 0)),
    )
    return pl.pallas_call(
        _expert_kernel,
        out_shape=jax.ShapeDtypeStruct((p_rows, d), F32),
        grid_spec=grid_spec,
        compiler_params=_cparams(("arbitrary",)),
        name="moe_experts",
    )(blk_lo, blk_hi, n_used, xs, w_gate, w_up, w_down, w_gate, w_up, w_down)


def _combine_kernel(ps_ref, rt_ref, rt_next_ref, x_ref, gt_ref, fg_ref, ys_ref, o_ref, ybuf, sems, *, final):
    i = pl.program_id(0)
    n_tok = x_ref.shape[0]
    slot = lax.rem(i, 2)

    def gather(rt, dst_slot):
        def issue(g, carry):
            for uu in range(DMA_UNROLL):
                r = g * DMA_UNROLL + uu
                _row_copy(ys_ref, _dest_row(ps_ref, rt, r), ybuf.at[dst_slot], r, sems.at[dst_slot]).start()
            return carry

        lax.fori_loop(0, n_tok // DMA_UNROLL, issue, 0)

    @pl.when(i == 0)
    def _():
        gather(rt_ref, 0)

    @pl.when(i + 1 < pl.num_programs(0))
    def _():
        gather(rt_next_ref, 1 - slot)

    pltpu.make_async_copy(ys_ref.at[pl.ds(0, n_tok), :], ybuf.at[slot], sems.at[slot]).wait()
    x = x_ref[...] + gt_ref[0] * ybuf[slot]
    if final:
        x = x * lax.rsqrt(jnp.mean(x * x, -1, keepdims=True) + NORM_EPS) * fg_ref[...]
    o_ref[...] = x


def _combine(pad_start, rt3, x, gt, final_g, ys, seq, final):
    t, d = x.shape
    tc = MOE_TOK
    per_b = seq // tc
    n_steps = t // tc
    grid_spec = pltpu.PrefetchScalarGridSpec(
        num_scalar_prefetch=1,
        grid=(n_steps,),
        in_specs=[pl.BlockSpec((1, 1, 2 * tc), lambda i, ps: (i, 0, 0), memory_space=pltpu.SMEM),
                  pl.BlockSpec((1, 1, 2 * tc), lambda i, ps: (jnp.minimum(i + 1, n_steps - 1), 0, 0),
                               memory_space=pltpu.SMEM),
                  pl.BlockSpec((tc, d), lambda i, ps: (i, 0)),
                  pl.BlockSpec((1, 1, d), lambda i, ps: (i // per_b, 0, 0)),
                  pl.BlockSpec((1, d), lambda i, ps: (0, 0)),
                  pl.BlockSpec(memory_space=pl.ANY)],
        out_specs=pl.BlockSpec((tc, d), lambda i, ps: (i, 0)),
        scratch_shapes=[pltpu.VMEM((2, tc, d), F32), pltpu.SemaphoreType.DMA((2,))],
    )
    return pl.pallas_call(
        functools.partial(_combine_kernel, final=final),
        out_shape=jax.ShapeDtypeStruct((t, d), F32),
        grid_spec=grid_spec,
        compiler_params=_cparams(("arbitrary",)),
        name="moe_combine",
    )(pad_start, rt3, rt3, x, gt, final_g, ys)


def _bucket_experts():
    lo, hi = [], []
    for g in range(N_GROUPS):
        for a in range(EXPERTS_PER_GROUP):
            for b in range(a + 1, EXPERTS_PER_GROUP):
                lo.append(g * EXPERTS_PER_GROUP + a)
                hi.append(g * EXPERTS_PER_GROUP + b)
    return jnp.asarray(lo, jnp.int32), jnp.asarray(hi, jnp.int32)


def _route_plan(route, counts, t):
    bm = MOE_ROWS
    cnt = counts[0, :N_BUCKETS].astype(jnp.int32)
    padded = (cnt + bm - 1) // bm * bm
    pad_end = jnp.cumsum(padded)
    pad_start = pad_end - padded
    p_rows = t + N_BUCKETS * bm
    n_blk = p_rows // bm
    blk_start = jnp.arange(n_blk, dtype=jnp.int32) * bm
    blk_b = jnp.minimum(jnp.sum((pad_end[None, :] <= blk_start[:, None]).astype(jnp.int32), axis=1), N_BUCKETS - 1)
    lo_tab, hi_tab = _bucket_experts()
    n_used = (pad_end[-1:] // bm).astype(jnp.int32)
    rt3 = route[:, 0:2].astype(jnp.int32).reshape(t // MOE_TOK, 1, 2 * MOE_TOK)
    return pad_start, rt3, lo_tab[blk_b], hi_tab[blk_b], n_used, p_rows


def kernel(x, c, ada_w, ada_b, mix_norm_g, ffn_norm_g, w_in, w_out, gdn_conv_w, gdn_a_log, gdn_dt_bias, gdn_norm_g,
           sgu_ln_g, sgu_ln_b, sgu_w, sgu_b, sc_conv_w, rw_mu, rw_w0, rw_w_up, rw_a0, rw_a_up, rw_g_up, rw_k_k,
           rw_k_a, rw_r_k, rw_gn_g, rw_gn_b, moe_w_group, moe_b_group, moe_w_router, moe_b_router, moe_w_gate,
           moe_w_up, moe_w_down, final_norm_g):
    bn, seq, d = x.shape
    depth = ada_w.shape[0]
    t = bn * seq
    assert d == 4 * GROUP_W and seq % ROW_BLOCK == 0 and ROW_BLOCK % MOE_TOK == 0
    g_w = GROUP_W
    mod = _ada(c, ada_w, ada_b)
    wg_b, wu_b, wd_b = _cast_expert_weights(moe_w_gate, moe_w_up, moe_w_down)
    xf = x.reshape(t, d)
    o_z, o_a, o_su = 3 * g_w, 4 * g_w, 4 * g_w + 2 * N_HEADS
    o_rp = o_su + 5 * g_w
    for l in range(depth):
        m = mod[l].reshape(bn, 6, 1, d)
        sh_m, sc_m, gt_m, sh_f, sc_f, gt_f = (m[:, i] for i in range(6))
        wl = w_in[l]
        w_ab = wl[:, o_a:o_su]
        w_r = jnp.concatenate([wl[:, 0:o_a], wl[:, o_su:o_rp], wl[:, o_rp:],
                               jnp.pad(w_ab, ((0, 0), (0, LANES - 2 * N_HEADS)))], axis=1).astype(BF16)
        pg, psc, prw, pab, abt = _in_proj(xf, mix_norm_g[l][None, :], sh_m, sc_m, w_r, w_ab.T.astype(BF16), seq)
        oa = _gdn(pg, pab, abt, gdn_conv_w[l], gdn_a_log[l], gdn_dt_bias[l], gdn_norm_g[l], bn, seq)
        obc = _sgu_conv(psc, sgu_ln_g[l], sgu_ln_b[l], sgu_w[l], sgu_b[l], sc_conv_w[l], bn, seq)
        od = _rwkv(prw, rw_mu[l], rw_w0[l], rw_w_up[l], rw_a0[l], rw_a_up[l], rw_g_up[l], rw_k_k[l], rw_k_a[l],
                   rw_r_k[l], rw_gn_g[l], rw_gn_b[l], bn, seq)
        w_rt = jnp.concatenate([moe_w_router[l], moe_w_group[l],
                                jnp.zeros((d, LANES - N_EXPERTS - N_GROUPS), F32)], axis=1)
        wr_hi = w_rt.astype(BF16)
        wr_lo = (w_rt - wr_hi.astype(F32)).astype(BF16)
        b_r = jnp.concatenate([moe_b_router[l], moe_b_group[l], jnp.zeros((LANES - N_EXPERTS - N_GROUPS,), F32)])[None, :]
        xf, hf, route, counts = _out_router(xf, oa, obc, od, w_out[l].astype(BF16), gt_m, ffn_norm_g[l][None, :],
                                            sh_f, sc_f, wr_hi, wr_lo, b_r, seq)
        pad_start, rt3, blk_lo, blk_hi, n_used, p_rows = _route_plan(route, counts, t)
        xs = _dispatch(pad_start, rt3, hf, p_rows)
        ys = _experts(blk_lo, blk_hi, n_used, xs, wg_b, wu_b, wd_b, l)
        xf = _combine(pad_start, rt3, xf, gt_f, final_norm_g[None, :], ys, seq, final=(l == depth - 1))
    return xf.reshape(bn, seq, d)
```
